```python
import math
import jax, jax.numpy as jnp
from jax import lax
import numpy as np

D_MODEL = 1024
BATCH = 32
SEQ = 2048
DEPTH = 1

CTX_LEN = 256
GRID_W = 64
D_INNER = 2 * D_MODEL
ATTN_WIDTH = D_INNER // 2
SSD_WIDTH = D_INNER - ATTN_WIDTH
HEAD_DIM = 64
N_Q_HEADS = ATTN_WIDTH // HEAD_DIM
N_KV_HEADS = 4
Q_PER_KV = N_Q_HEADS // N_KV_HEADS
KV_WIDTH = N_KV_HEADS * HEAD_DIM
SSD_HEAD_DIM = 64
SSD_HEADS = SSD_WIDTH // SSD_HEAD_DIM
SSD_GROUPS = 2
SSD_HEADS_PER_GROUP = SSD_HEADS // SSD_GROUPS
D_STATE = 128
GN = SSD_GROUPS * D_STATE
D_CONV = 5
CHUNK = 128
Q_BLOCK = 128
ROPE_THETA = 10000.0
ATTN_SCALE = HEAD_DIM ** -0.5
EPS = 1e-6
CONV_CH = SSD_WIDTH + 2 * GN
OFF_V = KV_WIDTH
OFF_XBC = 2 * KV_WIDTH
OFF_DT = OFF_XBC + CONV_CH
CTX_COLS = OFF_DT + 2 * SSD_HEADS
PROJ_COLS = CTX_COLS + 2 * ATTN_WIDTH + SSD_WIDTH

kernel_name = "hymba_ssd_gqa_axial_dit_layer"


def rms_norm(x, g):
    xf = x.astype(jnp.float32)
    y = xf * lax.rsqrt(jnp.mean(xf * xf, axis=-1, keepdims=True) + EPS)
    return (y * g.astype(jnp.float32)).astype(x.dtype)


def axial_rope(x):
    L = x.shape[1]
    rows = L // GRID_W
    row, col = jnp.meshgrid(jnp.arange(rows, dtype=jnp.float32),
                            jnp.arange(GRID_W, dtype=jnp.float32), indexing="ij")
    n_freq = HEAD_DIM // 4
    inv_freq = ROPE_THETA ** (-jnp.arange(n_freq, dtype=jnp.float32) / n_freq)
    ang = jnp.stack([row.reshape(-1)[:, None] * inv_freq,
                     col.reshape(-1)[:, None] * inv_freq], axis=1)
    cos = jnp.cos(ang)[:, None].astype(x.dtype)
    sin = jnp.sin(ang)[:, None].astype(x.dtype)
    xr = x.reshape(*x.shape[:3], 2, 2, n_freq)
    x1, x2 = xr[..., 0, :], xr[..., 1, :]
    out = jnp.stack([x1 * cos - x2 * sin, x2 * cos + x1 * sin], axis=-2)
    return out.reshape(x.shape)


def dwconv_centred(u, w, b):
    pad = D_CONV // 2
    y = lax.conv_general_dilated(u, w[:, None, :].astype(u.dtype), window_strides=(1,),
                                 padding=[(pad, pad)], dimension_numbers=("NWC", "WIO", "NWC"),
                                 feature_group_count=u.shape[-1])
    return y + b


def ssd_inputs(xbc_raw, dt_raw, conv_w, conv_b, dt_bias):
    bsz, L = xbc_raw.shape[:2]
    xbc = jax.nn.silu(dwconv_centred(xbc_raw, conv_w, conv_b))
    xs = xbc[..., :SSD_WIDTH].reshape(bsz, L, SSD_HEADS, SSD_HEAD_DIM)
    bm = xbc[..., SSD_WIDTH:SSD_WIDTH + GN].reshape(bsz, L, SSD_GROUPS, D_STATE)
    cm = xbc[..., SSD_WIDTH + GN:].reshape(bsz, L, SSD_GROUPS, D_STATE)
    dt = jax.nn.softplus(dt_raw.reshape(bsz, L, 2, SSD_HEADS).astype(jnp.float32) + dt_bias)
    return xs, bm, cm, dt


def ssd_scan(xs, dt, a_neg, b_in, c_in, init_state, want_y):
    bsz, L = xs.shape[:2]
    nc = L // CHUNK
    X = (xs * dt[..., None]).reshape(bsz, nc, CHUNK, SSD_GROUPS, SSD_HEADS_PER_GROUP, SSD_HEAD_DIM)
    a = (dt * a_neg).reshape(bsz, nc, CHUNK, SSD_GROUPS, SSD_HEADS_PER_GROUP)
    a_cs = jnp.cumsum(jnp.transpose(a, (0, 3, 4, 1, 2)), axis=-1)
    bc = b_in.reshape(bsz, nc, CHUNK, SSD_GROUPS, D_STATE)
    cc = c_in.reshape(bsz, nc, CHUNK, SSD_GROUPS, D_STATE)
    decay_to_end = jnp.exp(a_cs[..., -1:] - a_cs)
    chunk_states = jnp.einsum("bcsgn,bgrcs,bcsgrp->bcgrpn", bc, decay_to_end, X)
    chunk_decay = jnp.exp(a_cs[..., -1])

    def step(state, inp):
        st, dec = inp
        return state * dec[..., None, None] + st, state

    final, entering = lax.scan(step, init_state.astype(chunk_states.dtype),
                               (jnp.moveaxis(chunk_states, 1, 0), jnp.moveaxis(chunk_decay, -1, 0)))
    if not want_y:
        return None, final
    entering = jnp.moveaxis(entering, 0, 1)
    tril = jnp.tril(jnp.ones((CHUNK, CHUNK), dtype=bool))
    seg = a_cs[..., :, None] - a_cs[..., None, :]
    lmat = jnp.exp(jnp.where(tril, seg, -jnp.inf))
    cb = jnp.einsum("bclgn,bcsgn->bgcls", cc, bc)
    y_diag = jnp.einsum("bgrcls,bcsgrp->bclgrp", cb[:, :, None] * lmat, X)
    y_off = jnp.einsum("bclgn,bgrcl,bcgrpn->bclgrp", cc, jnp.exp(a_cs), entering)
    y = (y_diag + y_off).reshape(bsz, L, SSD_HEADS, SSD_HEAD_DIM)
    return y, final


def bidir_ssd(ssd_c, ssd_l, a_neg, want_ctx_y):
    xs_c, b_c, c_c, dt_c = ssd_c
    xs_l, b_l, c_l, dt_l = ssd_l
    bsz = xs_l.shape[0]
    zero = jnp.zeros((bsz, SSD_GROUPS, SSD_HEADS_PER_GROUP, SSD_HEAD_DIM, D_STATE), jnp.float32)
    y_l = 0.0
    y_c = 0.0 if want_ctx_y else None
    for d in range(2):
        fl = (lambda t: jnp.flip(t, axis=1)) if d == 1 else (lambda t: t)
        yc, s_ctx = ssd_scan(fl(xs_c), fl(dt_c[:, :, d]), a_neg[d], fl(b_c), fl(c_c), zero, want_ctx_y)
        yl, _ = ssd_scan(fl(xs_l), fl(dt_l[:, :, d]), a_neg[d], fl(b_l), fl(c_l), s_ctx, True)
        y_l = y_l + fl(yl)
        if want_ctx_y:
            y_c = y_c + fl(yc)
    return y_l, y_c


def gqa_dense(q, k, v):
    bsz, lq = q.shape[:2]
    qg = q.reshape(bsz, lq, N_KV_HEADS, Q_PER_KV, HEAD_DIM)
    s = jnp.einsum("bqkrd,bskd->bkrqs", qg, k, preferred_element_type=jnp.float32) * ATTN_SCALE
    p = jax.nn.softmax(s, axis=-1).astype(v.dtype)
    o = jnp.einsum("bkrqs,bskd->bqkrd", p, v)
    return o.reshape(bsz, lq, ATTN_WIDTH)


def blocked_attention(q, k, v):
    bsz, L = q.shape[:2]
    nb = L // Q_BLOCK
    qb = jnp.moveaxis(q.reshape(bsz, nb, Q_BLOCK, N_Q_HEADS, HEAD_DIM), 1, 0)
    o = lax.map(lambda qi: gqa_dense(qi, k, v), qb)
    return jnp.moveaxis(o, 0, 1).reshape(bsz, L, ATTN_WIDTH)


def split_ctx_cols(p):
    k = p[..., :OFF_V]
    v = p[..., OFF_V:OFF_XBC]
    xbc = p[..., OFF_XBC:OFF_DT]
    dt = p[..., OFF_DT:CTX_COLS]
    return k, v, xbc, dt


def split_query_cols(p):
    q = p[..., CTX_COLS:CTX_COLS + ATTN_WIDTH]
    g_attn = p[..., CTX_COLS + ATTN_WIDTH:CTX_COLS + 2 * ATTN_WIDTH]
    z = p[..., CTX_COLS + 2 * ATTN_WIDTH:]
    return q, g_attn, z


def merge_heads(o_attn, g_attn, y_ssd, z, ssd_norm_g, w_out):
    y_a = o_attn * jax.nn.silu(g_attn)
    y_s = rms_norm(y_ssd * jax.nn.silu(z), ssd_norm_g)
    return jnp.concatenate([y_a, y_s], axis=-1) @ w_out


def hybrid_layer(x_lat, x_ctx, c, c_ctx, ada_w, ada_b, norm_g, w_in, conv_w, conv_b, dt_bias,
                 a_log, d_skip, q_norm_g, k_norm_g, ssd_norm_g, w_out, ctx_out):
    bsz, L = x_lat.shape[:2]
    lc = x_ctx.shape[1]
    mod_l = jax.nn.silu(c) @ ada_w + ada_b
    mod_c = jax.nn.silu(c_ctx) @ ada_w + ada_b
    sh_l, sc_l, g_l = jnp.split(mod_l[:, None, :], 3, axis=-1)
    sh_c, sc_c, g_c = jnp.split(mod_c, 3, axis=-1)
    h_l = rms_norm(x_lat, norm_g) * (1 + sc_l) + sh_l
    h_c = rms_norm(x_ctx, norm_g) * (1 + sc_c) + sh_c
    p_l = h_l @ w_in
    p_c = h_c @ (w_in if ctx_out else w_in[:, :CTX_COLS])
    k_l, v_l, xbc_l, dt_l = split_ctx_cols(p_l)
    k_c, v_c, xbc_c, dt_c = split_ctx_cols(p_c)
    q_l, ga_l, z_l = split_query_cols(p_l)

    k_l = axial_rope(rms_norm(k_l.reshape(bsz, L, N_KV_HEADS, HEAD_DIM), k_norm_g))
    k_c = rms_norm(k_c.reshape(bsz, lc, N_KV_HEADS, HEAD_DIM), k_norm_g)
    v_l = v_l.reshape(bsz, L, N_KV_HEADS, HEAD_DIM)
    v_c = v_c.reshape(bsz, lc, N_KV_HEADS, HEAD_DIM)
    q_l = axial_rope(rms_norm(q_l.reshape(bsz, L, N_Q_HEADS, HEAD_DIM), q_norm_g))
    o_l = blocked_attention(q_l, jnp.concatenate([k_l, k_c], axis=1), jnp.concatenate([v_l, v_c], axis=1))

    ssd_c = ssd_inputs(xbc_c, dt_c, conv_w, conv_b, dt_bias)
    ssd_l = ssd_inputs(xbc_l, dt_l, conv_w, conv_b, dt_bias)
    a_neg = -jnp.exp(a_log.astype(jnp.float32))
    y_l, y_c = bidir_ssd(ssd_c, ssd_l, a_neg, ctx_out)
    y_l = y_l + d_skip[:, None] * ssd_l[0]

    out_l = merge_heads(o_l, ga_l, y_l.reshape(bsz, L, SSD_WIDTH).astype(x_lat.dtype), z_l, ssd_norm_g, w_out)
    new_lat = x_lat + g_l * out_l

    new_ctx = x_ctx
    if ctx_out:
        q_c, ga_c, z_c = split_query_cols(p_c)
        q_c = rms_norm(q_c.reshape(bsz, lc, N_Q_HEADS, HEAD_DIM), q_norm_g)
        o_c = gqa_dense(q_c, k_c, v_c)
        y_c = y_c + d_skip[:, None] * ssd_c[0]
        out_c = merge_heads(o_c, ga_c, y_c.reshape(bsz, lc, SSD_WIDTH).astype(x_ctx.dtype), z_c, ssd_norm_g, w_out)
        new_ctx = x_ctx + g_c * out_c
    return new_lat, new_ctx


def setup_inputs(seed: int = 0) -> dict:
    key = jax.random.key(seed)
    ks = jax.random.split(key, 20)
    f32 = jnp.float32
    nrm = lambda k, shape, s: jax.random.normal(k, shape, f32) * s
    dt0 = jnp.exp(jax.random.uniform(ks[9], (DEPTH, 2, SSD_HEADS), f32, math.log(1e-3), math.log(1e-1)))
    return {
        "x": nrm(ks[0], (BATCH, SEQ, D_MODEL), 1.0),
        "c": nrm(ks[1], (BATCH, D_MODEL), 1.0),
        "ctx": nrm(ks[2], (BATCH, CTX_LEN, D_MODEL), 1.0),
        "c_ctx": nrm(ks[3], (D_MODEL,), 1.0),
        "ada_w": nrm(ks[4], (DEPTH, D_MODEL, 3 * D_MODEL), 0.5 * D_MODEL ** -0.5),
        "ada_b": nrm(ks[5], (DEPTH, 3 * D_MODEL), 0.01),
        "norm_g": 1.0 + nrm(ks[6], (DEPTH, D_MODEL), 0.05),
        "w_in": nrm(ks[7], (DEPTH, D_MODEL, PROJ_COLS), D_MODEL ** -0.5),
        "conv_w": nrm(ks[8], (DEPTH, D_CONV, CONV_CH), D_CONV ** -0.5),
        "conv_b": nrm(ks[10], (DEPTH, CONV_CH), 0.01),
        "dt_bias": dt0 + jnp.log(-jnp.expm1(-dt0)),
        "a_log": jnp.log(jax.random.uniform(ks[11], (DEPTH, 2, SSD_HEADS), f32, 1.0, 16.0)),
        "d_skip": 1.0 + nrm(ks[12], (DEPTH, SSD_HEADS), 0.05),
        "q_norm_g": 1.0 + nrm(ks[13], (DEPTH, HEAD_DIM), 0.05),
        "k_norm_g": 1.0 + nrm(ks[14], (DEPTH, HEAD_DIM), 0.05),
        "ssd_norm_g": 1.0 + nrm(ks[15], (DEPTH, SSD_WIDTH), 0.05),
        "w_out": nrm(ks[16], (DEPTH, D_INNER, D_MODEL), D_INNER ** -0.5),
        "final_g": 1.0 + nrm(ks[17], (D_MODEL,), 0.05),
    }


def reference(x, c, ctx, c_ctx, ada_w, ada_b, norm_g, w_in, conv_w, conv_b, dt_bias, a_log, d_skip,
              q_norm_g, k_norm_g, ssd_norm_g, w_out, final_g):
    x_lat, x_ctx = x, ctx
    for layer in range(DEPTH):
        ctx_out = layer < DEPTH - 1
        x_lat, x_ctx = hybrid_layer(x_lat, x_ctx, c, c_ctx, ada_w[layer], ada_b[layer], norm_g[layer],
                                    w_in[layer], conv_w[layer], conv_b[layer], dt_bias[layer],
                                    a_log[layer], d_skip[layer], q_norm_g[layer], k_norm_g[layer],
                                    ssd_norm_g[layer], w_out[layer], ctx_out)
    return rms_norm(x_lat, final_g)
```

```python
import functools

import jax
import jax.numpy as jnp
import numpy as np
from jax import lax
from jax.experimental import pallas as pl
from jax.experimental.pallas import tpu as pltpu

F32 = jnp.float32
BF16 = jnp.bfloat16

D_MODEL = 1024
GRID_W = 64
HEAD_DIM = 64
N_Q_HEADS = 16
N_KV_HEADS = 4
Q_PER_KV = N_Q_HEADS // N_KV_HEADS
ATTN_WIDTH = N_Q_HEADS * HEAD_DIM
KV_WIDTH = N_KV_HEADS * HEAD_DIM
SSD_HEADS = 16
SSD_HEAD_DIM = 64
SSD_WIDTH = SSD_HEADS * SSD_HEAD_DIM
SSD_GROUPS = 2
GROUP_WIDTH = SSD_WIDTH // SSD_GROUPS
D_STATE = 128
GN = SSD_GROUPS * D_STATE
D_CONV = 5
CONV_PAD = D_CONV // 2
CONV_CH = SSD_WIDTH + 2 * GN
CHUNK = 128
ROPE_THETA = 10000.0
ATTN_SCALE = HEAD_DIM ** -0.5
EPS = 1e-6
OFF_V = KV_WIDTH
OFF_XBC = 2 * KV_WIDTH
OFF_DT = OFF_XBC + CONV_CH
CTX_COLS = OFF_DT + 2 * SSD_HEADS
OFF_Q = CTX_COLS
OFF_GA = OFF_Q + ATTN_WIDTH
OFF_Z = OFF_GA + ATTN_WIDTH

LANES = 128
DT_PAD = LANES
CONV_HALO = 16
VMEM_LIMIT = 56 * 1024 * 1024

IN_TILE = 512
OUT_TILE = 512
ATTN_TILE = 256


def _silu(v):
    return v * (1.0 / (1.0 + jnp.exp(-v)))


def _softplus(v):
    return jnp.maximum(v, 0.0) + jnp.log1p(jnp.exp(-jnp.abs(v)))


def _split3(v):
    p1 = v.astype(BF16)
    r1 = v - p1.astype(F32)
    p2 = r1.astype(BF16)
    p3 = (r1 - p2.astype(F32)).astype(BF16)
    return p1, p2, p3


def _dot(a, b):
    return jnp.dot(a, b, preferred_element_type=F32)


def _dot_exact_rhs(sel, v):
    p1, p2, p3 = _split3(v)
    return _dot(sel, p1) + _dot(sel, p2) + _dot(sel, p3)


def _dot_exact_lhs(v, sel):
    p1, p2, p3 = _split3(v)
    return _dot(p1, sel) + _dot(p2, sel) + _dot(p3, sel)


def _const_spec(shape):
    nd = len(shape)
    return pl.BlockSpec(shape, lambda *_: (0,) * nd, pipeline_mode=pl.Buffered(1))


def _mod_kernel(c_ref, w_ref, b_ref, o_ref):
    s = _silu(c_ref[...])
    o_ref[...] = jnp.dot(s, w_ref[...], preferred_element_type=F32,
                         precision=lax.Precision.HIGHEST) + b_ref[...]


def _mod_call(cc, ada_w, ada_b):
    rows = cc.shape[0]
    n = ada_w.shape[1]
    bn = D_MODEL
    return pl.pallas_call(
        _mod_kernel,
        grid=(n // bn,),
        in_specs=[pl.BlockSpec((rows, D_MODEL), lambda j: (0, 0)),
                  pl.BlockSpec((D_MODEL, bn), lambda j: (0, j)),
                  pl.BlockSpec((1, bn), lambda j: (0, j))],
        out_specs=pl.BlockSpec((rows, bn), lambda j: (0, j)),
        out_shape=jax.ShapeDtypeStruct((rows, n), F32),
        compiler_params=pltpu.CompilerParams(dimension_semantics=("arbitrary",),
                                             vmem_limit_bytes=VMEM_LIMIT),
        name="adaln_mod",
    )(cc, ada_w, ada_b.reshape(1, n))


def _modulated_norm(x, mod, ng):
    ms = jnp.mean(x * x, axis=-1, keepdims=True)
    y = (x * lax.rsqrt(ms + EPS)) * ng
    return y * (1.0 + mod[:, D_MODEL:2 * D_MODEL]) + mod[:, :D_MODEL]


def _head_norm(v, gain, bd):
    ssq = _dot((v * v).astype(BF16), bd)
    return v * lax.rsqrt(ssq * (1.0 / HEAD_DIM) + EPS) * gain


def _rope(v, cos, sin_signed):
    lane = lax.broadcasted_iota(jnp.int32, v.shape, 1)
    up = pltpu.roll(v, LANES - 16, axis=1)
    down = pltpu.roll(v, 16, axis=1)
    partner = jnp.where((lane & 16) == 0, up, down)
    return v * cos + partner * sin_signed


def _in_kernel_common(x_ref, mod_ref, ng_ref, wk_ref, wv_ref, wx_ref, wdt_ref, dtb_ref, kg_ref, bd_ref):
    h = _modulated_norm(x_ref[0], mod_ref[0], ng_ref[...]).astype(BF16)
    kn = _head_norm(_dot(h, wk_ref[...]), kg_ref[...], bd_ref[...])
    v = _dot(h, wv_ref[...]).astype(BF16)
    xbc = _dot(h, wx_ref[...]).astype(BF16)
    dt_raw = _dot(h, wdt_ref[...]) + dtb_ref[...]
    lane = lax.broadcasted_iota(jnp.int32, dt_raw.shape, 1)
    dt = jnp.where(lane < 2 * SSD_HEADS, _softplus(dt_raw), 0.0)
    return h, kn, v, xbc, dt


def _in_kernel_ctx(x_ref, mod_ref, ng_ref, wk_ref, wv_ref, wx_ref, wdt_ref, dtb_ref, kg_ref, bd_ref,
                   kt_out, v_out, xbc_out, dt_out):
    _, kn, v, xbc, dt = _in_kernel_common(x_ref, mod_ref, ng_ref, wk_ref, wv_ref, wx_ref, wdt_ref,
                                          dtb_ref, kg_ref, bd_ref)
    kt_out[0] = kn.T.astype(BF16)
    v_out[0] = v
    xbc_out[0] = xbc
    dt_out[0] = dt


def _in_kernel_lat(x_ref, mod_ref, ng_ref, wk_ref, wv_ref, wx_ref, wdt_ref, dtb_ref, kg_ref, bd_ref,
                   wq_ref, wg_ref, wz_ref, qg_ref, cos_ref, sin_ref,
                   kt_out, v_out, xbc_out, dt_out, q_out, ga_out, z_out, *, tile):
    h, kn, v, xbc, dt = _in_kernel_common(x_ref, mod_ref, ng_ref, wk_ref, wv_ref, wx_ref, wdt_ref,
                                          dtb_ref, kg_ref, bd_ref)
    row0 = pl.multiple_of(pl.program_id(1) * tile, tile)
    cos = cos_ref[pl.ds(row0, tile), :]
    sin = sin_ref[pl.ds(row0, tile), :]
    k_rot = jnp.concatenate([_rope(kn[:, s * LANES:(s + 1) * LANES], cos, sin)
                             for s in range(KV_WIDTH // LANES)], axis=1)
    kt_out[0] = k_rot.T.astype(BF16)
    v_out[0] = v
    xbc_out[0] = xbc
    dt_out[0] = dt
    for s in range(ATTN_WIDTH // KV_WIDTH):
        cols = slice(s * KV_WIDTH, (s + 1) * KV_WIDTH)
        qn = _head_norm(_dot(h, wq_ref[:, cols]), qg_ref[...], bd_ref[...])
        q_rot = jnp.concatenate([_rope(qn[:, t * LANES:(t + 1) * LANES], cos, sin)
                                 for t in range(KV_WIDTH // LANES)], axis=1)
        q_out[0, :, cols] = (q_rot * ATTN_SCALE).astype(BF16)
    ga_out[0] = _dot(h, wg_ref[...]).astype(BF16)
    z_out[0] = _dot(h, wz_ref[...]).astype(BF16)


def _in_proj_call(x, mod, ng, w, consts, *, latent, tile):
    bsz, rows, _ = x.shape
    nt = rows // tile
    tok = lambda width: pl.BlockSpec((1, tile, width), lambda b, j: (b, j, 0))
    mod_map = (lambda b, j: (b, 0, 0)) if latent else (lambda b, j: (0, 0, 0))
    in_specs = [tok(D_MODEL),
                pl.BlockSpec((1, 1, 3 * D_MODEL), mod_map),
                _const_spec((1, D_MODEL)),
                _const_spec((D_MODEL, KV_WIDTH)), _const_spec((D_MODEL, KV_WIDTH)),
                _const_spec((D_MODEL, CONV_CH)), _const_spec((D_MODEL, DT_PAD)),
                _const_spec((1, DT_PAD)), _const_spec((1, KV_WIDTH)), _const_spec((KV_WIDTH, KV_WIDTH))]
    args = [x, mod, ng, w["k"], w["v"], w["xbc"], w["dt"], consts["dt_bias"], consts["k_gain"], consts["bd"]]
    out_specs = [pl.BlockSpec((1, KV_WIDTH, tile), lambda b, j: (b, 0, j)),
                 tok(KV_WIDTH), tok(CONV_CH), tok(DT_PAD)]
    out_shape = [jax.ShapeDtypeStruct((bsz, KV_WIDTH, rows), BF16),
                 jax.ShapeDtypeStruct((bsz, rows, KV_WIDTH), BF16),
                 jax.ShapeDtypeStruct((bsz, rows, CONV_CH), BF16),
                 jax.ShapeDtypeStruct((bsz, rows, DT_PAD), F32)]
    if latent:
        in_specs += [_const_spec((D_MODEL, ATTN_WIDTH)), _const_spec((D_MODEL, ATTN_WIDTH)),
                     _const_spec((D_MODEL, SSD_WIDTH)), _const_spec((1, KV_WIDTH)),
                     _const_spec((rows, LANES)), _const_spec((rows, LANES))]
        args += [w["q"], w["ga"], w["z"], consts["q_gain"], consts["cos"], consts["sin"]]
        out_specs += [tok(ATTN_WIDTH), tok(ATTN_WIDTH), tok(SSD_WIDTH)]
        out_shape += [jax.ShapeDtypeStruct((bsz, rows, ATTN_WIDTH), BF16),
                      jax.ShapeDtypeStruct((bsz, rows, ATTN_WIDTH), BF16),
                      jax.ShapeDtypeStruct((bsz, rows, SSD_WIDTH), BF16)]
        body = functools.partial(_in_kernel_lat, tile=tile)
    else:
        body = _in_kernel_ctx
    return pl.pallas_call(
        body,
        grid=(bsz, nt),
        in_specs=in_specs,
        out_specs=out_specs,
        out_shape=out_shape,
        compiler_params=pltpu.CompilerParams(dimension_semantics=("arbitrary", "arbitrary"),
                                             vmem_limit_bytes=VMEM_LIMIT),
        name="in_proj_latent" if latent else "in_proj_context",
    )(*args)


def _attn_kernel(q_ref, ktl_ref, ktc_ref, vl_ref, vc_ref, o_ref):
    vl = vl_ref[0]
    vc = vc_ref[0]
    for kv in range(N_KV_HEADS):
        kv_rows = slice(kv * HEAD_DIM, (kv + 1) * HEAD_DIM)
        ktl = ktl_ref[0, kv_rows, :]
        ktc = ktc_ref[0, kv_rows, :]
        outs = []
        for r in range(Q_PER_KV):
            hq = kv * Q_PER_KV + r
            qh = q_ref[0, :, hq * HEAD_DIM:(hq + 1) * HEAD_DIM]
            s_l = _dot(qh, ktl)
            s_c = _dot(qh, ktc)
            m = jnp.maximum(jnp.max(s_l, axis=-1, keepdims=True), jnp.max(s_c, axis=-1, keepdims=True))
            p_l = jnp.exp(s_l - m)
            p_c = jnp.exp(s_c - m)
            denom = jnp.sum(p_l, axis=-1, keepdims=True) + jnp.sum(p_c, axis=-1, keepdims=True)
            o = _dot(p_l.astype(BF16), vl) + _dot(p_c.astype(BF16), vc)
            outs.append(o[:, kv_rows] * (1.0 / denom))
        o_ref[0, :, kv * KV_WIDTH:(kv + 1) * KV_WIDTH] = jnp.concatenate(outs, axis=1).astype(BF16)


def _attn_call(q, kt_l, kt_c, v_l, v_c, *, tile):
    bsz, rows, _ = q.shape
    lc = v_c.shape[1]
    per_batch = lambda shape: pl.BlockSpec((1,) + shape, lambda b, j: (b, 0, 0))
    return pl.pallas_call(
        _attn_kernel,
        grid=(bsz, rows // tile),
        in_specs=[pl.BlockSpec((1, tile, ATTN_WIDTH), lambda b, j: (b, j, 0)),
                  per_batch((KV_WIDTH, rows)), per_batch((KV_WIDTH, lc)),
                  per_batch((rows, KV_WIDTH)), per_batch((lc, KV_WIDTH))],
        out_specs=pl.BlockSpec((1, tile, ATTN_WIDTH), lambda b, j: (b, j, 0)),
        out_shape=jax.ShapeDtypeStruct((bsz, rows, ATTN_WIDTH), BF16),
        compiler_params=pltpu.CompilerParams(dimension_semantics=("arbitrary", "arbitrary"),
                                             vmem_limit_bytes=VMEM_LIMIT),
        name="gqa_attention",
    )(q, kt_l, kt_c, v_l, v_c)


def _conv_chunk(src_ref, dst_ref, cw_ref, cb_ref, i, n_chunks):
    i = jnp.asarray(i, jnp.int32)
    row0 = pl.multiple_of(i * CHUNK, CHUNK)
    main = src_ref[0, pl.ds(row0, CHUNK), :].astype(F32)
    prev0 = pl.multiple_of(jnp.maximum(row0 - CONV_HALO, 0), CONV_HALO)
    next0 = pl.multiple_of(jnp.minimum(row0 + CHUNK, (n_chunks - 1) * CHUNK + CHUNK - CONV_HALO), CONV_HALO)
    prev = src_ref[0, pl.ds(prev0, CONV_HALO), :].astype(F32) * (i > 0).astype(F32)
    nxt = src_ref[0, pl.ds(next0, CONV_HALO), :].astype(F32) * (i < n_chunks - 1).astype(F32)
    win = jnp.concatenate([prev, main, nxt], axis=0)
    acc = cb_ref[...] + jnp.zeros((CHUNK, CONV_CH), F32)
    for k in range(D_CONV):
        start = CONV_HALO + k - CONV_PAD
        acc = acc + cw_ref[k:k + 1, :] * win[start:start + CHUNK, :]
    dst_ref[pl.ds(row0, CHUNK), :] = _silu(acc).astype(BF16)


def _ssd_chunk(act_ref, dt_ref, row0, direction, want_y, aneg, tri, expand, state_ref):
    rows = pl.ds(row0, CHUNK)
    dtc = dt_ref[0, rows, :]
    a_cs = _dot_exact_rhs(tri, dtc * aneg)
    last = CHUNK - 1 if direction == 0 else 0
    dt_b = _dot_exact_lhs(dtc, expand)
    acs_b = _dot_exact_lhs(a_cs, expand)
    atot_b = acs_b[last:last + 1, :]
    xs = act_ref[rows, :SSD_WIDTH].astype(F32)
    bm = act_ref[rows, SSD_WIDTH:SSD_WIDTH + GN]
    x_dt = xs * dt_b
    x_end = (x_dt * jnp.exp(atot_b - acs_b)).astype(BF16)
    state = state_ref[...]
    made = [lax.dot_general(bm[:, g * D_STATE:(g + 1) * D_STATE],
                            x_end[:, g * GROUP_WIDTH:(g + 1) * GROUP_WIDTH],
                            (((0,), (0,)), ((), ())), preferred_element_type=F32)
            for g in range(SSD_GROUPS)]
    state_ref[...] = state * jnp.exp(atot_b) + jnp.concatenate(made, axis=1)
    if not want_y:
        return None
    cm = act_ref[rows, SSD_WIDTH + GN:]
    x_bf = x_dt.astype(BF16)
    state_bf = state.astype(BF16)
    decay_in = jnp.exp(acs_b)
    a_cs_t = a_cs.T
    li = lax.broadcasted_iota(jnp.int32, (CHUNK, CHUNK), 0)
    si = lax.broadcasted_iota(jnp.int32, (CHUNK, CHUNK), 1)
    visible = (si <= li) if direction == 0 else (si >= li)
    slab_lane = lax.broadcasted_iota(jnp.int32, (CHUNK, KV_WIDTH), 1)
    heads_per_group = SSD_HEADS // SSD_GROUPS
    heads_per_slab = KV_WIDTH // SSD_HEAD_DIM
    slabs = []
    for g in range(SSD_GROUPS):
        bg = bm[:, g * D_STATE:(g + 1) * D_STATE]
        cg = cm[:, g * D_STATE:(g + 1) * D_STATE]
        gcols = slice(g * GROUP_WIDTH, (g + 1) * GROUP_WIDTH)
        y_off = _dot(cg, state_bf[:, gcols]) * decay_in[:, gcols]
        cb = lax.dot_general(cg, bg, (((1,), (1,)), ((), ())), preferred_element_type=F32)
        for sl in range(GROUP_WIDTH // KV_WIDTH):
            slab0 = g * GROUP_WIDTH + sl * KV_WIDTH
            x_slab = x_bf[:, slab0:slab0 + KV_WIDTH]
            acc = y_off[:, sl * KV_WIDTH:(sl + 1) * KV_WIDTH]
            for t in range(heads_per_slab):
                hd = g * heads_per_group + sl * heads_per_slab + t
                lane = direction * SSD_HEADS + hd
                seg = a_cs[:, lane:lane + 1] - a_cs_t[lane:lane + 1, :]
                lmat = jnp.exp(jnp.where(visible, seg, -jnp.inf))
                in_head = (slab_lane >= t * SSD_HEAD_DIM) & (slab_lane < (t + 1) * SSD_HEAD_DIM)
                x_head = jnp.where(in_head, x_slab, jnp.zeros_like(x_slab))
                acc = acc + _dot((cb * lmat).astype(BF16), x_head)
            slabs.append(acc)
    return jnp.concatenate(slabs, axis=1)


def _ssd_kernel(xl_ref, dtl_ref, xc_ref, dtc_ref, cw_ref, cb_ref, alog_ref, dskip_ref, tril_ref, triu_ref,
                e_fwd_ref, e_bwd_ref, y_ref, act_l, act_c, y_fwd, state_ref, *, n_lat, n_ctx):
    lax.fori_loop(0, n_lat, lambda i, c: (_conv_chunk(xl_ref, act_l, cw_ref, cb_ref, i, n_lat), c)[1], 0)
    for i in range(n_ctx):
        _conv_chunk(xc_ref, act_c, cw_ref, cb_ref, i, n_ctx)
    aneg = -jnp.exp(alog_ref[...])
    for direction in range(2):
        tri = (tril_ref if direction == 0 else triu_ref)[...]
        expand = (e_fwd_ref if direction == 0 else e_bwd_ref)[...]
        order = (lambda i, n: i) if direction == 0 else (lambda i, n: n - 1 - i)
        state_ref[...] = jnp.zeros_like(state_ref)
        for i in range(n_ctx):
            _ssd_chunk(act_c, dtc_ref, order(i, n_ctx) * CHUNK, direction, False, aneg, tri, expand, state_ref)

        def body(i, carry, direction=direction, tri=tri, expand=expand, order=order):
            row0 = pl.multiple_of(order(i, n_lat) * CHUNK, CHUNK)
            y = _ssd_chunk(act_l, dtl_ref, row0, direction, True, aneg, tri, expand, state_ref)
            rows = pl.ds(row0, CHUNK)
            if direction == 0:
                y_fwd[rows, :] = y
            else:
                xs = act_l[rows, :SSD_WIDTH].astype(F32)
                y_ref[0, rows, :] = (y_fwd[rows, :] + y + dskip_ref[...] * xs).astype(BF16)
            return carry

        lax.fori_loop(0, n_lat, body, 0)


def _ssd_call(xbc_l, dt_l, xbc_c, dt_c, consts):
    bsz, rows, _ = xbc_l.shape
    lc = xbc_c.shape[1]
    per_batch = lambda shape: pl.BlockSpec((1,) + shape, lambda b: (b, 0, 0))
    body = functools.partial(_ssd_kernel, n_lat=rows // CHUNK, n_ctx=lc // CHUNK)
    return pl.pallas_call(
        body,
        grid=(bsz,),
        in_specs=[per_batch((rows, CONV_CH)), per_batch((rows, DT_PAD)),
                  per_batch((lc, CONV_CH)), per_batch((lc, DT_PAD)),
                  _const_spec((8, CONV_CH)), _const_spec((1, CONV_CH)), _const_spec((1, DT_PAD)),
                  _const_spec((1, SSD_WIDTH)), _const_spec((CHUNK, CHUNK)), _const_spec((CHUNK, CHUNK)),
                  _const_spec((DT_PAD, SSD_WIDTH)), _const_spec((DT_PAD, SSD_WIDTH))],
        out_specs=per_batch((rows, SSD_WIDTH)),
        out_shape=jax.ShapeDtypeStruct((bsz, rows, SSD_WIDTH), BF16),
        scratch_shapes=[pltpu.VMEM((rows, CONV_CH), BF16), pltpu.VMEM((lc, CONV_CH), BF16),
                        pltpu.VMEM((rows, SSD_WIDTH), F32), pltpu.VMEM((D_STATE, SSD_WIDTH), F32)],
        compiler_params=pltpu.CompilerParams(dimension_semantics=("arbitrary",),
                                             vmem_limit_bytes=VMEM_LIMIT),
        name="ssd_bidir",
    )(xbc_l, dt_l, xbc_c, dt_c, consts["conv_w"], consts["conv_b"], consts["a_log"], consts["d_skip"],
      consts["tril"], consts["triu"], consts["e_fwd"], consts["e_bwd"])


def _out_kernel(o_ref, ga_ref, y_ref, z_ref, x_ref, mod_ref, wa_ref, ws_ref, sg_ref, fg_ref, out_ref):
    y_a = (o_ref[0].astype(F32) * _silu(ga_ref[0].astype(F32))).astype(BF16)
    t = y_ref[0].astype(F32) * _silu(z_ref[0].astype(F32))
    y_s = (t * lax.rsqrt(jnp.mean(t * t, axis=-1, keepdims=True) + EPS) * sg_ref[...]).astype(BF16)
    proj = _dot(y_a, wa_ref[...]) + _dot(y_s, ws_ref[...])
    new = x_ref[0] + mod_ref[0][:, 2 * D_MODEL:] * proj
    out_ref[0] = new * lax.rsqrt(jnp.mean(new * new, axis=-1, keepdims=True) + EPS) * fg_ref[...]


def _out_call(o_attn, ga, y_ssd, z, x, mod, w_a, w_s, ssd_gain, final_gain, *, tile):
    bsz, rows, _ = x.shape
    tok = pl.BlockSpec((1, tile, D_MODEL), lambda b, j: (b, j, 0))
    return pl.pallas_call(
        _out_kernel,
        grid=(bsz, rows // tile),
        in_specs=[tok, tok, tok, tok, tok,
                  pl.BlockSpec((1, 1, 3 * D_MODEL), lambda b, j: (b, 0, 0)),
                  _const_spec((ATTN_WIDTH, D_MODEL)), _const_spec((SSD_WIDTH, D_MODEL)),
                  _const_spec((1, SSD_WIDTH)), _const_spec((1, D_MODEL))],
        out_specs=tok,
        out_shape=jax.ShapeDtypeStruct((bsz, rows, D_MODEL), F32),
        compiler_params=pltpu.CompilerParams(dimension_semantics=("arbitrary", "arbitrary"),
                                             vmem_limit_bytes=VMEM_LIMIT),
        name="merge_out_proj",
    )(o_attn, ga, y_ssd, z, x, mod, w_a, w_s, ssd_gain, final_gain)


def _rope_tables(rows):
    n_freq = HEAD_DIM // 4
    t = jnp.arange(rows, dtype=jnp.int32)
    pos = jnp.stack([(t // GRID_W).astype(F32), (t % GRID_W).astype(F32)], axis=1)
    inv_freq = ROPE_THETA ** (-jnp.arange(n_freq, dtype=F32) / n_freq)
    ang = pos[:, :, None] * inv_freq
    cos = jnp.cos(ang)[:, :, None, :]
    sin = jnp.sin(ang)[:, :, None, :]
    cos_h = jnp.broadcast_to(cos, (rows, 2, 2, n_freq)).reshape(rows, HEAD_DIM)
    sin_h = jnp.concatenate([-sin, sin], axis=2).reshape(rows, HEAD_DIM)
    reps = LANES // HEAD_DIM
    return jnp.tile(cos_h, (1, reps)), jnp.tile(sin_h, (1, reps))


def _pad_lanes(v, width):
    return jnp.pad(v, [(0, 0)] * (v.ndim - 1) + [(0, width - v.shape[-1])])


def kernel(x, c, ctx, c_ctx, ada_w, ada_b, norm_g, w_in, conv_w, conv_b, dt_bias, a_log, d_skip,
           q_norm_g, k_norm_g, ssd_norm_g, w_out, final_g):
    assert ada_w.shape[0] == 1, "single-layer problem: context outputs are never needed"
    bsz, rows, _ = x.shape
    ada_w, ada_b, norm_g, w_in, conv_w, conv_b = ada_w[0], ada_b[0], norm_g[0], w_in[0], conv_w[0], conv_b[0]
    dt_bias, a_log, d_skip = dt_bias[0], a_log[0], d_skip[0]
    q_norm_g, k_norm_g, ssd_norm_g, w_out = q_norm_g[0], k_norm_g[0], ssd_norm_g[0], w_out[0]

    n_rows = -(-(bsz + 1) // 8) * 8
    cc = jnp.concatenate([c, c_ctx[None, :], jnp.zeros((n_rows - bsz - 1, D_MODEL), F32)], axis=0)
    mod = _mod_call(cc, ada_w, ada_b)
    mod_l = mod[:bsz].reshape(bsz, 1, 3 * D_MODEL)
    mod_c = mod[bsz:bsz + 1].reshape(1, 1, 3 * D_MODEL)

    w_bf = w_in.astype(BF16)
    weights = {
        "k": w_bf[:, :OFF_V], "v": w_bf[:, OFF_V:OFF_XBC], "xbc": w_bf[:, OFF_XBC:OFF_DT],
        "dt": _pad_lanes(w_bf[:, OFF_DT:CTX_COLS], DT_PAD),
        "q": w_bf[:, OFF_Q:OFF_GA], "ga": w_bf[:, OFF_GA:OFF_Z], "z": w_bf[:, OFF_Z:],
    }
    cos, sin = _rope_tables(rows)
    head_of_lane = np.arange(KV_WIDTH) // HEAD_DIM
    consts = {
        "dt_bias": _pad_lanes(dt_bias.reshape(1, 2 * SSD_HEADS), DT_PAD),
        "k_gain": jnp.tile(k_norm_g, KV_WIDTH // HEAD_DIM).reshape(1, KV_WIDTH),
        "q_gain": jnp.tile(q_norm_g, KV_WIDTH // HEAD_DIM).reshape(1, KV_WIDTH),
        "bd": jnp.asarray(head_of_lane[:, None] == head_of_lane[None, :], BF16),
        "cos": cos, "sin": sin,
    }
    ng = norm_g.reshape(1, D_MODEL)
    kt_c, v_c, xbc_c, dt_c = _in_proj_call(ctx, mod_c, ng, weights, consts, latent=False,
                                           tile=min(IN_TILE, ctx.shape[1]))
    kt_l, v_l, xbc_l, dt_l, q, ga, z = _in_proj_call(x, mod_l, ng, weights, consts, latent=True, tile=IN_TILE)

    o_attn = _attn_call(q, kt_l, kt_c, v_l, v_c, tile=ATTN_TILE)

    tok = np.arange(CHUNK)
    lane_head = np.arange(SSD_WIDTH) // SSD_HEAD_DIM
    dt_lane = np.arange(DT_PAD)
    ssd_consts = {
        "conv_w": jnp.pad(conv_w, ((0, 8 - D_CONV), (0, 0))),
        "conv_b": conv_b.reshape(1, CONV_CH),
        "a_log": _pad_lanes(a_log.reshape(1, 2 * SSD_HEADS), DT_PAD),
        "d_skip": jnp.repeat(d_skip, SSD_HEAD_DIM).reshape(1, SSD_WIDTH),
        "tril": jnp.asarray(tok[None, :] <= tok[:, None], BF16),
        "triu": jnp.asarray(tok[None, :] >= tok[:, None], BF16),
        "e_fwd": jnp.asarray(dt_lane[:, None] == lane_head[None, :], BF16),
        "e_bwd": jnp.asarray(dt_lane[:, None] == lane_head[None, :] + SSD_HEADS, BF16),
    }
    y_ssd = _ssd_call(xbc_l, dt_l, xbc_c, dt_c, ssd_consts)

    w_out_bf = w_out.astype(BF16)
    return _out_call(o_attn, ga, y_ssd, z, x, mod_l, w_out_bf[:ATTN_WIDTH], w_out_bf[ATTN_WIDTH:],
                     ssd_norm_g.reshape(1, SSD_WIDTH), final_g.reshape(1, D_MODEL), tile=OUT_TILE)
```

```python
import functools

import jax
import jax.numpy as jnp
import numpy as np
from jax import lax
from jax.experimental import pallas as pl
from jax.experimental.pallas import tpu as pltpu

F32 = jnp.float32
BF16 = jnp.bfloat16

D_MODEL = 1024
GRID_W = 64
HEAD_DIM = 64
N_Q_HEADS = 16
N_KV_HEADS = 4
Q_PER_KV = N_Q_HEADS // N_KV_HEADS
ATTN_WIDTH = N_Q_HEADS * HEAD_DIM
KV_WIDTH = N_KV_HEADS * HEAD_DIM
SSD_HEADS = 16
SSD_HEAD_DIM = 64
SSD_WIDTH = SSD_HEADS * SSD_HEAD_DIM
SSD_GROUPS = 2
GROUP_WIDTH = SSD_WIDTH // SSD_GROUPS
D_STATE = 128
GN = SSD_GROUPS * D_STATE
D_CONV = 5
CONV_PAD = D_CONV // 2
CONV_CH = SSD_WIDTH + 2 * GN
CHUNK = 128
ROPE_THETA = 10000.0
ATTN_SCALE = HEAD_DIM ** -0.5
EPS = 1e-6
OFF_V = KV_WIDTH
OFF_XBC = 2 * KV_WIDTH
OFF_DT = OFF_XBC + CONV_CH
CTX_COLS = OFF_DT + 2 * SSD_HEADS
OFF_Q = CTX_COLS
OFF_GA = OFF_Q + ATTN_WIDTH
OFF_Z = OFF_GA + ATTN_WIDTH

LANES = 128
DT_PAD = LANES
CONV_HALO = 16
CONV_WINDOW = 256
PIECE_LANES = 2 * SSD_HEADS
LOG2E = 1.4426950408889634
VMEM_LIMIT = 56 * 1024 * 1024

IN_TILE = 1024
OUT_TILE = 1024
ATTN_TILE = 256
REDUCE_PARTS = 8
HEADS_PER_PASS = 4
PREP_UNROLL = 2
SCAN_UNROLL = 2


def _silu(v):
    return v * (1.0 / (1.0 + jnp.exp2(v * (-LOG2E))))


def _softplus(v):
    return jnp.maximum(v, 0.0) + jnp.log1p(jnp.exp(-jnp.abs(v)))


def _split3(v):
    p1 = v.astype(BF16)
    r1 = v - p1.astype(F32)
    p2 = r1.astype(BF16)
    p3 = (r1 - p2.astype(F32)).astype(BF16)
    return p1, p2, p3


def _dot(a, b):
    return jnp.dot(a, b, preferred_element_type=F32)


def _dot_exact_rhs(sel, v):
    p1, p2, p3 = _split3(v)
    return _dot(sel, p1) + _dot(sel, p2) + _dot(sel, p3)


def _dot_exact_lhs(v, sel):
    p1, p2, p3 = _split3(v)
    return _dot(p1, sel) + _dot(p2, sel) + _dot(p3, sel)


def _const_spec(shape):
    nd = len(shape)
    return pl.BlockSpec(shape, lambda *_: (0,) * nd, pipeline_mode=pl.Buffered(1))


def _mod_kernel(c_ref, w_ref, b_ref, o_ref):
    s = _silu(c_ref[...])
    o_ref[...] = jnp.dot(s, w_ref[...], preferred_element_type=F32,
                         precision=lax.Precision.HIGHEST) + b_ref[...]


def _mod_call(cc, ada_w, ada_b):
    rows = cc.shape[0]
    n = ada_w.shape[1]
    bn = D_MODEL
    return pl.pallas_call(
        _mod_kernel,
        grid=(n // bn,),
        in_specs=[pl.BlockSpec((rows, D_MODEL), lambda j: (0, 0)),
                  pl.BlockSpec((D_MODEL, bn), lambda j: (0, j)),
                  pl.BlockSpec((1, bn), lambda j: (0, j))],
        out_specs=pl.BlockSpec((rows, bn), lambda j: (0, j)),
        out_shape=jax.ShapeDtypeStruct((rows, n), F32),
        compiler_params=pltpu.CompilerParams(dimension_semantics=("arbitrary",),
                                             vmem_limit_bytes=VMEM_LIMIT),
        name="adaln_mod",
    )(cc, ada_w, ada_b.reshape(1, n))


def _modulated_norm(x, mod, ng):
    ms = jnp.mean(x * x, axis=-1, keepdims=True)
    y = (x * lax.rsqrt(ms + EPS)) * ng
    return y * (1.0 + mod[:, D_MODEL:2 * D_MODEL]) + mod[:, :D_MODEL]


def _head_norm(v, gain, bd):
    ssq = _dot((v * v).astype(BF16), bd)
    return v * lax.rsqrt(ssq * (1.0 / HEAD_DIM) + EPS) * gain


def _rope(v, cos, sin_signed):
    lane = lax.broadcasted_iota(jnp.int32, v.shape, 1)
    up = pltpu.roll(v, LANES - 16, axis=1)
    down = pltpu.roll(v, 16, axis=1)
    partner = jnp.where((lane & 16) == 0, up, down)
    return v * cos + partner * sin_signed


def _in_kernel_common(x_ref, mod_ref, ng_ref, wk_ref, wv_ref, wx_ref, wdt_ref, dtb_ref, kg_ref, bd_ref):
    h = _modulated_norm(x_ref[0], mod_ref[0], ng_ref[...]).astype(BF16)
    kn = _head_norm(_dot(h, wk_ref[...]), kg_ref[...], bd_ref[...])
    v = _dot(h, wv_ref[...])
    xbc = _dot(h, wx_ref[...]).astype(BF16)
    dt_raw = _dot(h, wdt_ref[...]) + dtb_ref[...]
    lane = lax.broadcasted_iota(jnp.int32, dt_raw.shape, 1)
    dt = jnp.where(lane < 2 * SSD_HEADS, _softplus(dt_raw), 0.0)
    return h, kn, v, xbc, dt


def _in_kernel_ctx(x_ref, mod_ref, ng_ref, wk_ref, wv_ref, wx_ref, wdt_ref, dtb_ref, kg_ref, bd_ref,
                   vt_out, k_out, xbc_out, dt_out):
    _, kn, v, xbc, dt = _in_kernel_common(x_ref, mod_ref, ng_ref, wk_ref, wv_ref, wx_ref, wdt_ref,
                                          dtb_ref, kg_ref, bd_ref)
    vt_out[0] = v.T.astype(BF16)
    k_out[0] = kn.astype(BF16)
    xbc_out[0] = xbc
    dt_out[0] = dt


def _in_kernel_lat(x_ref, mod_ref, ng_ref, wk_ref, wv_ref, wx_ref, wdt_ref, dtb_ref, kg_ref, bd_ref,
                   wq_ref, wg_ref, wz_ref, qg_ref, cos_ref, sin_ref,
                   vt_out, k_out, xbc_out, dt_out, q_out, ga_out, z_out, *, tile):
    h, kn, v, xbc, dt = _in_kernel_common(x_ref, mod_ref, ng_ref, wk_ref, wv_ref, wx_ref, wdt_ref,
                                          dtb_ref, kg_ref, bd_ref)
    row0 = pl.multiple_of(pl.program_id(1) * tile, tile)
    cos = cos_ref[pl.ds(row0, tile), :]
    sin = sin_ref[pl.ds(row0, tile), :]
    k_rot = jnp.concatenate([_rope(kn[:, s * LANES:(s + 1) * LANES], cos, sin)
                             for s in range(KV_WIDTH // LANES)], axis=1)
    vt_out[0] = v.T.astype(BF16)
    k_out[0] = k_rot.astype(BF16)
    xbc_out[0] = xbc
    dt_out[0] = dt
    for s in range(ATTN_WIDTH // KV_WIDTH):
        cols = slice(s * KV_WIDTH, (s + 1) * KV_WIDTH)
        qn = _head_norm(_dot(h, wq_ref[:, cols]), qg_ref[...], bd_ref[...])
        q_rot = jnp.concatenate([_rope(qn[:, t * LANES:(t + 1) * LANES], cos, sin)
                                 for t in range(KV_WIDTH // LANES)], axis=1)
        q_out[0, :, cols] = (q_rot * (ATTN_SCALE * LOG2E)).astype(BF16)
    ga_out[0] = _dot(h, wg_ref[...]).astype(BF16)
    z_out[0] = _dot(h, wz_ref[...]).astype(BF16)


def _in_proj_call(x, mod, ng, w, consts, *, latent, tile):
    bsz, rows, _ = x.shape
    nt = rows // tile
    tok = lambda width: pl.BlockSpec((1, tile, width), lambda b, j: (b, j, 0))
    mod_map = (lambda b, j: (b, 0, 0)) if latent else (lambda b, j: (0, 0, 0))
    in_specs = [tok(D_MODEL),
                pl.BlockSpec((1, 1, 3 * D_MODEL), mod_map),
                _const_spec((1, D_MODEL)),
                _const_spec((D_MODEL, KV_WIDTH)), _const_spec((D_MODEL, KV_WIDTH)),
                _const_spec((D_MODEL, CONV_CH)), _const_spec((D_MODEL, DT_PAD)),
                _const_spec((1, DT_PAD)), _const_spec((1, KV_WIDTH)), _const_spec((KV_WIDTH, KV_WIDTH))]
    args = [x, mod, ng, w["k"], w["v"], w["xbc"], w["dt"], consts["dt_bias"], consts["k_gain"], consts["bd"]]
    out_specs = [pl.BlockSpec((1, KV_WIDTH, tile), lambda b, j: (b, 0, j)),
                 tok(KV_WIDTH), tok(CONV_CH), tok(DT_PAD)]
    out_shape = [jax.ShapeDtypeStruct((bsz, KV_WIDTH, rows), BF16),
                 jax.ShapeDtypeStruct((bsz, rows, KV_WIDTH), BF16),
                 jax.ShapeDtypeStruct((bsz, rows, CONV_CH), BF16),
                 jax.ShapeDtypeStruct((bsz, rows, DT_PAD), F32)]
    if latent:
        in_specs += [_const_spec((D_MODEL, ATTN_WIDTH)), _const_spec((D_MODEL, ATTN_WIDTH)),
                     _const_spec((D_MODEL, SSD_WIDTH)), _const_spec((1, KV_WIDTH)),
                     _const_spec((rows, LANES)), _const_spec((rows, LANES))]
        args += [w["q"], w["ga"], w["z"], consts["q_gain"], consts["cos"], consts["sin"]]
        out_specs += [tok(ATTN_WIDTH), tok(ATTN_WIDTH), tok(SSD_WIDTH)]
        out_shape += [jax.ShapeDtypeStruct((bsz, rows, ATTN_WIDTH), BF16),
                      jax.ShapeDtypeStruct((bsz, rows, ATTN_WIDTH), BF16),
                      jax.ShapeDtypeStruct((bsz, rows, SSD_WIDTH), BF16)]
        body = functools.partial(_in_kernel_lat, tile=tile)
    else:
        body = _in_kernel_ctx
    return pl.pallas_call(
        body,
        grid=(bsz, nt),
        in_specs=in_specs,
        out_specs=out_specs,
        out_shape=out_shape,
        compiler_params=pltpu.CompilerParams(dimension_semantics=("arbitrary", "arbitrary"),
                                             vmem_limit_bytes=VMEM_LIMIT),
        name="in_proj_latent" if latent else "in_proj_context",
    )(*args)


def _split_reduce(op, a):
    parts = a.reshape(REDUCE_PARTS, a.shape[0] // REDUCE_PARTS, a.shape[1])
    return op(op(parts, axis=1), axis=0, keepdims=True)


def _attn_kernel(q_ref, kl_ref, kc_ref, vtl_ref, vtc_ref, o_ref):
    tile = q_ref.shape[1]
    q_t = q_ref[0].T
    k_l = kl_ref[0]
    k_c = kc_ref[0]
    zeros = jnp.zeros((HEAD_DIM, tile), BF16)
    outs = []
    for kv in range(N_KV_HEADS):
        kv_rows = slice(kv * HEAD_DIM, (kv + 1) * HEAD_DIM)
        vt_l = vtl_ref[0, kv_rows, :]
        vt_c = vtc_ref[0, kv_rows, :]
        for r0 in range(0, Q_PER_KV, HEADS_PER_PASS):
            q_pad = jnp.concatenate(
                [jnp.concatenate([zeros] * kv + [q_t[hq * HEAD_DIM:(hq + 1) * HEAD_DIM, :]]
                                 + [zeros] * (N_KV_HEADS - 1 - kv), axis=0)
                 for hq in range(kv * Q_PER_KV + r0, kv * Q_PER_KV + r0 + HEADS_PER_PASS)], axis=1)
            s_l = _dot(k_l, q_pad)
            s_c = _dot(k_c, q_pad)
            m = jnp.maximum(_split_reduce(jnp.max, s_l), jnp.max(s_c, axis=0, keepdims=True))
            p_l = jnp.exp2(s_l - m)
            p_c = jnp.exp2(s_c - m)
            denom = _split_reduce(jnp.sum, p_l) + jnp.sum(p_c, axis=0, keepdims=True)
            o_t = _dot(vt_l, p_l.astype(BF16)) + _dot(vt_c, p_c.astype(BF16))
            o_t = o_t * (1.0 / denom)
            outs += [o_t[:, t * tile:(t + 1) * tile] for t in range(HEADS_PER_PASS)]
    o_ref[0] = jnp.concatenate(outs, axis=0).T.astype(BF16)


def _attn_call(q, k_l, k_c, vt_l, vt_c, *, tile):
    bsz, rows, _ = q.shape
    lc = k_c.shape[1]
    per_batch = lambda shape: pl.BlockSpec((1,) + shape, lambda b, j: (b, 0, 0))
    return pl.pallas_call(
        _attn_kernel,
        grid=(bsz, rows // tile),
        in_specs=[pl.BlockSpec((1, tile, ATTN_WIDTH), lambda b, j: (b, j, 0)),
                  per_batch((rows, KV_WIDTH)), per_batch((lc, KV_WIDTH)),
                  per_batch((KV_WIDTH, rows)), per_batch((KV_WIDTH, lc))],
        out_specs=pl.BlockSpec((1, tile, ATTN_WIDTH), lambda b, j: (b, j, 0)),
        out_shape=jax.ShapeDtypeStruct((bsz, rows, ATTN_WIDTH), BF16),
        compiler_params=pltpu.CompilerParams(dimension_semantics=("arbitrary", "arbitrary"),
                                             vmem_limit_bytes=VMEM_LIMIT),
        name="gqa_attention",
    )(q, k_l, k_c, vt_l, vt_c)


def _conv_chunk(src_ref, dst_ref, cw_ref, cb_ref, shift_ref, i, n_chunks):
    i = jnp.asarray(i, jnp.int32)
    row0 = pl.multiple_of(i * CHUNK, CHUNK)
    main = src_ref[0, pl.ds(row0, CHUNK), :]
    prev0 = pl.multiple_of(jnp.maximum(row0 - CONV_HALO, 0), CONV_HALO)
    next0 = pl.multiple_of(jnp.minimum(row0 + CHUNK, (n_chunks - 1) * CHUNK + CHUNK - CONV_HALO), CONV_HALO)
    prev = src_ref[0, pl.ds(prev0, CONV_HALO), :] * (i > 0).astype(BF16)
    nxt = src_ref[0, pl.ds(next0, CONV_HALO), :] * (i < n_chunks - 1).astype(BF16)
    fill = jnp.zeros((CONV_WINDOW - CHUNK - 2 * CONV_HALO, CONV_CH), BF16)
    win = jnp.concatenate([prev, main, nxt, fill], axis=0)
    off_taps = [k for k in range(D_CONV) if k != CONV_PAD]
    for c0 in range(0, CONV_CH, GROUP_WIDTH):
        cols = slice(c0, c0 + GROUP_WIDTH)
        shifted = _dot(shift_ref[...], win[:, cols])
        acc = cb_ref[:, cols] + cw_ref[CONV_PAD:CONV_PAD + 1, cols] * main[:, cols].astype(F32)
        for j, k in enumerate(off_taps):
            acc = acc + cw_ref[k:k + 1, cols] * shifted[j * CHUNK:(j + 1) * CHUNK, :]
        dst_ref[pl.ds(row0, CHUNK), cols] = _silu(acc).astype(BF16)


def _pack3(v):
    p1, p2, p3 = _split3(v)
    packed = p1.astype(F32) + pltpu.roll(p2.astype(F32), PIECE_LANES, axis=1) \
        + pltpu.roll(p3.astype(F32), 2 * PIECE_LANES, axis=1)
    return packed.astype(BF16)


def _chunk_terms(dt_ref, terms, i, aneg2, tri2, lane_ok, lane_fwd, piece_fwd):
    acs_ref, pk_dt_ref, pk_end_ref, pk_start_ref = terms
    rows = pl.ds(pl.multiple_of(jnp.asarray(i, jnp.int32) * CHUNK, CHUNK), CHUNK)
    dtc = dt_ref[0, rows, :]
    r = _dot(tri2, _pack3(dtc * aneg2))
    r = jnp.where(piece_fwd, r[:CHUNK], r[CHUNK:])
    a_cs = (r + pltpu.roll(r, LANES - PIECE_LANES, axis=1) + pltpu.roll(r, LANES - 2 * PIECE_LANES, axis=1)) * lane_ok
    total = jnp.where(lane_fwd, a_cs[CHUNK - 1:CHUNK, :], a_cs[0:1, :])
    acs_ref[rows, :] = a_cs
    pk_dt_ref[rows, :] = _pack3(dtc)
    pk_end_ref[rows, :] = _pack3(dtc * jnp.exp2(total - a_cs) * lane_ok)
    pk_start_ref[rows, :] = _pack3(jnp.exp2(a_cs) * lane_ok)


def _chunk_cb(act_ref, cb_ref, i):
    rows = pl.ds(pl.multiple_of(jnp.asarray(i, jnp.int32) * CHUNK, CHUNK), CHUNK)
    for g in range(SSD_GROUPS):
        bg = act_ref[rows, SSD_WIDTH + g * D_STATE:SSD_WIDTH + (g + 1) * D_STATE]
        cg = act_ref[rows, SSD_WIDTH + GN + g * D_STATE:SSD_WIDTH + GN + (g + 1) * D_STATE]
        cb_ref[rows, g * CHUNK:(g + 1) * CHUNK] = lax.dot_general(
            cg, bg, (((1,), (1,)), ((), ())), preferred_element_type=F32)


def _ssd_chunk(act_ref, terms, cb_ref, row0, direction, want_y, expand, state_ref):
    acs_ref, pk_dt_ref, pk_end_ref, pk_start_ref = terms
    rows = pl.ds(row0, CHUNK)
    last = CHUNK - 1 if direction == 0 else 0
    a_cs = acs_ref[rows, :]
    dt_end_b = _dot(pk_end_ref[rows, :], expand)
    start_b = _dot(pk_start_ref[rows, :], expand)
    chunk_decay_b = start_b[last:last + 1, :]
    xs = act_ref[rows, :SSD_WIDTH].astype(F32)
    bm = act_ref[rows, SSD_WIDTH:SSD_WIDTH + GN]
    x_end = (xs * dt_end_b).astype(BF16)
    state = state_ref[...]
    made = [lax.dot_general(bm[:, g * D_STATE:(g + 1) * D_STATE],
                            x_end[:, g * GROUP_WIDTH:(g + 1) * GROUP_WIDTH],
                            (((0,), (0,)), ((), ())), preferred_element_type=F32)
            for g in range(SSD_GROUPS)]
    state_ref[...] = state * chunk_decay_b + jnp.concatenate(made, axis=1)
    if not want_y:
        return None
    cm = act_ref[rows, SSD_WIDTH + GN:]
    x_bf = (xs * _dot(pk_dt_ref[rows, :], expand)).astype(BF16)
    state_bf = state.astype(BF16)
    a_cs_t = a_cs.T
    li = lax.broadcasted_iota(jnp.int32, (CHUNK, CHUNK), 0)
    si = lax.broadcasted_iota(jnp.int32, (CHUNK, CHUNK), 1)
    visible = (si <= li) if direction == 0 else (si >= li)
    slab_lane = lax.broadcasted_iota(jnp.int32, (CHUNK, KV_WIDTH), 1)
    heads_per_group = SSD_HEADS // SSD_GROUPS
    heads_per_slab = KV_WIDTH // SSD_HEAD_DIM
    slabs = []
    for g in range(SSD_GROUPS):
        cg = cm[:, g * D_STATE:(g + 1) * D_STATE]
        gcols = slice(g * GROUP_WIDTH, (g + 1) * GROUP_WIDTH)
        y_off = _dot(cg, state_bf[:, gcols]) * start_b[:, gcols]
        cb = cb_ref[rows, g * CHUNK:(g + 1) * CHUNK]
        for sl in range(GROUP_WIDTH // KV_WIDTH):
            slab0 = g * GROUP_WIDTH + sl * KV_WIDTH
            x_slab = x_bf[:, slab0:slab0 + KV_WIDTH]
            acc = y_off[:, sl * KV_WIDTH:(sl + 1) * KV_WIDTH]
            for pair in range(heads_per_slab // 2):
                mats, xheads = [], []
                for t in (2 * pair, 2 * pair + 1):
                    hd = g * heads_per_group + sl * heads_per_slab + t
                    lane = direction * SSD_HEADS + hd
                    seg = a_cs[:, lane:lane + 1] - a_cs_t[lane:lane + 1, :]
                    lmat = jnp.exp2(jnp.where(visible, seg, -jnp.inf))
                    mats.append((cb * lmat).astype(BF16))
                    in_head = (slab_lane >= t * SSD_HEAD_DIM) & (slab_lane < (t + 1) * SSD_HEAD_DIM)
                    xheads.append(jnp.where(in_head, x_slab, jnp.zeros_like(x_slab)))
                acc = acc + _dot(jnp.concatenate(mats, axis=1), jnp.concatenate(xheads, axis=0))
            slabs.append(acc)
    return jnp.concatenate(slabs, axis=1)


def _ssd_kernel(xl_ref, dtl_ref, xc_ref, dtc_ref, cw_ref, cb_ref, alog_ref, dskip_ref, shift_ref, tri2_ref,
                e_fwd_ref, e_bwd_ref, y_ref, act_l, act_c, y_part, state_f, state_b,
                acs_l, pkdt_l, pkend_l, pkstart_l, acs_c, pkdt_c, pkend_c, pkstart_c, cbt_l, *, n_lat, n_ctx):
    aneg2 = -jnp.exp(alog_ref[...]) * LOG2E
    dt_lane = lax.broadcasted_iota(jnp.int32, (1, DT_PAD), 1)
    lane_ok = (dt_lane < 2 * SSD_HEADS).astype(F32)
    lane_fwd = dt_lane < SSD_HEADS
    piece_fwd = (dt_lane % PIECE_LANES) < SSD_HEADS
    terms_l = (acs_l, pkdt_l, pkend_l, pkstart_l)
    terms_c = (acs_c, pkdt_c, pkend_c, pkstart_c)

    def prepare(src_ref, dt_ref, act_ref, terms, cbt_ref, i, n_chunks):
        _conv_chunk(src_ref, act_ref, cw_ref, cb_ref, shift_ref, i, n_chunks)
        _chunk_terms(dt_ref, terms, i, aneg2, tri2_ref[...], lane_ok, lane_fwd, piece_fwd)
        if cbt_ref is not None:
            _chunk_cb(act_ref, cbt_ref, i)

    lax.fori_loop(0, n_lat, lambda i, c: (prepare(xl_ref, dtl_ref, act_l, terms_l, cbt_l, i, n_lat), c)[1], 0,
                  unroll=PREP_UNROLL)
    for i in range(n_ctx):
        prepare(xc_ref, dtc_ref, act_c, terms_c, None, i, n_ctx)
    expands = (e_fwd_ref[...], e_bwd_ref[...])
    states = (state_f, state_b)
    for direction in range(2):
        states[direction][...] = jnp.zeros_like(states[direction])
        for i in range(n_ctx):
            chunk = i if direction == 0 else n_ctx - 1 - i
            _ssd_chunk(act_c, terms_c, None, chunk * CHUNK, direction, False, expands[direction], states[direction])

    def step(i, second_half):
        for direction in range(2):
            chunk = i if direction == 0 else n_lat - 1 - i
            row0 = pl.multiple_of(chunk * CHUNK, CHUNK)
            y = _ssd_chunk(act_l, terms_l, cbt_l, row0, direction, True, expands[direction], states[direction])
            rows = pl.ds(row0, CHUNK)
            if second_half:
                xs = act_l[rows, :SSD_WIDTH].astype(F32)
                y_ref[0, rows, :] = (y_part[rows, :] + y + dskip_ref[...] * xs).astype(BF16)
            else:
                y_part[rows, :] = y

    half = n_lat // 2
    lax.fori_loop(0, half, lambda i, c: (step(i, False), c)[1], 0, unroll=SCAN_UNROLL)
    lax.fori_loop(half, n_lat, lambda i, c: (step(i, True), c)[1], 0, unroll=SCAN_UNROLL)


def _ssd_call(xbc_l, dt_l, xbc_c, dt_c, consts):
    bsz, rows, _ = xbc_l.shape
    lc = xbc_c.shape[1]
    per_batch = lambda shape: pl.BlockSpec((1,) + shape, lambda b: (b, 0, 0))
    assert rows % (2 * CHUNK) == 0 and lc % CHUNK == 0
    body = functools.partial(_ssd_kernel, n_lat=rows // CHUNK, n_ctx=lc // CHUNK)
    return pl.pallas_call(
        body,
        grid=(bsz,),
        in_specs=[per_batch((rows, CONV_CH)), per_batch((rows, DT_PAD)),
                  per_batch((lc, CONV_CH)), per_batch((lc, DT_PAD)),
                  _const_spec((8, CONV_CH)), _const_spec((1, CONV_CH)), _const_spec((1, DT_PAD)),
                  _const_spec((1, SSD_WIDTH)), _const_spec(((D_CONV - 1) * CHUNK, CONV_WINDOW)),
                  _const_spec((2 * CHUNK, CHUNK)),
                  _const_spec((DT_PAD, SSD_WIDTH)), _const_spec((DT_PAD, SSD_WIDTH))],
        out_specs=per_batch((rows, SSD_WIDTH)),
        out_shape=jax.ShapeDtypeStruct((bsz, rows, SSD_WIDTH), BF16),
        scratch_shapes=[pltpu.VMEM((rows, CONV_CH), BF16), pltpu.VMEM((lc, CONV_CH), BF16),
                        pltpu.VMEM((rows, SSD_WIDTH), F32),
                        pltpu.VMEM((D_STATE, SSD_WIDTH), F32), pltpu.VMEM((D_STATE, SSD_WIDTH), F32)]
                       + [pltpu.VMEM((n, DT_PAD), dt) for n in (rows, lc) for dt in (F32, BF16, BF16, BF16)]
                       + [pltpu.VMEM((rows, SSD_GROUPS * CHUNK), F32)],
        compiler_params=pltpu.CompilerParams(dimension_semantics=("arbitrary",),
                                             vmem_limit_bytes=VMEM_LIMIT),
        name="ssd_bidir",
    )(xbc_l, dt_l, xbc_c, dt_c, consts["conv_w"], consts["conv_b"], consts["a_log"], consts["d_skip"],
      consts["shift"], consts["tri2"], consts["e_fwd"], consts["e_bwd"])


def _out_kernel(o_ref, ga_ref, y_ref, z_ref, x_ref, mod_ref, wa_ref, ws_ref, sg_ref, fg_ref, out_ref):
    y_a = (o_ref[0].astype(F32) * _silu(ga_ref[0].astype(F32))).astype(BF16)
    t = y_ref[0].astype(F32) * _silu(z_ref[0].astype(F32))
    y_s = (t * lax.rsqrt(jnp.mean(t * t, axis=-1, keepdims=True) + EPS) * sg_ref[...]).astype(BF16)
    proj = _dot(y_a, wa_ref[...]) + _dot(y_s, ws_ref[...])
    new = x_ref[0] + mod_ref[0][:, 2 * D_MODEL:] * proj
    out_ref[0] = new * lax.rsqrt(jnp.mean(new * new, axis=-1, keepdims=True) + EPS) * fg_ref[...]


def _out_call(o_attn, ga, y_ssd, z, x, mod, w_a, w_s, ssd_gain, final_gain, *, tile):
    bsz, rows, _ = x.shape
    tok = pl.BlockSpec((1, tile, D_MODEL), lambda b, j: (b, j, 0))
    return pl.pallas_call(
        _out_kernel,
        grid=(bsz, rows // tile),
        in_specs=[tok, tok, tok, tok, tok,
                  pl.BlockSpec((1, 1, 3 * D_MODEL), lambda b, j: (b, 0, 0)),
                  _const_spec((ATTN_WIDTH, D_MODEL)), _const_spec((SSD_WIDTH, D_MODEL)),
                  _const_spec((1, SSD_WIDTH)), _const_spec((1, D_MODEL))],
        out_specs=tok,
        out_shape=jax.ShapeDtypeStruct((bsz, rows, D_MODEL), F32),
        compiler_params=pltpu.CompilerParams(dimension_semantics=("arbitrary", "arbitrary"),
                                             vmem_limit_bytes=VMEM_LIMIT),
        name="merge_out_proj",
    )(o_attn, ga, y_ssd, z, x, mod, w_a, w_s, ssd_gain, final_gain)


def _rope_tables(rows):
    n_freq = HEAD_DIM // 4
    t = jnp.arange(rows, dtype=jnp.int32)
    pos = jnp.stack([(t // GRID_W).astype(F32), (t % GRID_W).astype(F32)], axis=1)
    inv_freq = ROPE_THETA ** (-jnp.arange(n_freq, dtype=F32) / n_freq)
    ang = pos[:, :, None] * inv_freq
    cos = jnp.cos(ang)[:, :, None, :]
    sin = jnp.sin(ang)[:, :, None, :]
    cos_h = jnp.broadcast_to(cos, (rows, 2, 2, n_freq)).reshape(rows, HEAD_DIM)
    sin_h = jnp.concatenate([-sin, sin], axis=2).reshape(rows, HEAD_DIM)
    reps = LANES // HEAD_DIM
    return jnp.tile(cos_h, (1, reps)), jnp.tile(sin_h, (1, reps))


def _pad_lanes(v, width):
    return jnp.pad(v, [(0, 0)] * (v.ndim - 1) + [(0, width - v.shape[-1])])


def kernel(x, c, ctx, c_ctx, ada_w, ada_b, norm_g, w_in, conv_w, conv_b, dt_bias, a_log, d_skip,
           q_norm_g, k_norm_g, ssd_norm_g, w_out, final_g):
    assert ada_w.shape[0] == 1, "single-layer problem: context outputs are never needed"
    bsz, rows, _ = x.shape
    ada_w, ada_b, norm_g, w_in, conv_w, conv_b = ada_w[0], ada_b[0], norm_g[0], w_in[0], conv_w[0], conv_b[0]
    dt_bias, a_log, d_skip = dt_bias[0], a_log[0], d_skip[0]
    q_norm_g, k_norm_g, ssd_norm_g, w_out = q_norm_g[0], k_norm_g[0], ssd_norm_g[0], w_out[0]

    n_rows = -(-(bsz + 1) // 8) * 8
    cc = jnp.concatenate([c, c_ctx[None, :], jnp.zeros((n_rows - bsz - 1, D_MODEL), F32)], axis=0)
    mod = _mod_call(cc, ada_w, ada_b)
    mod_l = mod[:bsz].reshape(bsz, 1, 3 * D_MODEL)
    mod_c = mod[bsz:bsz + 1].reshape(1, 1, 3 * D_MODEL)

    w_bf = w_in.astype(BF16)
    weights = {
        "k": w_bf[:, :OFF_V], "v": w_bf[:, OFF_V:OFF_XBC], "xbc": w_bf[:, OFF_XBC:OFF_DT],
        "dt": _pad_lanes(w_bf[:, OFF_DT:CTX_COLS], DT_PAD),
        "q": w_bf[:, OFF_Q:OFF_GA], "ga": w_bf[:, OFF_GA:OFF_Z], "z": w_bf[:, OFF_Z:],
    }
    cos, sin = _rope_tables(rows)
    head_of_lane = np.arange(KV_WIDTH) // HEAD_DIM
    consts = {
        "dt_bias": _pad_lanes(dt_bias.reshape(1, 2 * SSD_HEADS), DT_PAD),
        "k_gain": jnp.tile(k_norm_g, KV_WIDTH // HEAD_DIM).reshape(1, KV_WIDTH),
        "q_gain": jnp.tile(q_norm_g, KV_WIDTH // HEAD_DIM).reshape(1, KV_WIDTH),
        "bd": jnp.asarray(head_of_lane[:, None] == head_of_lane[None, :], BF16),
        "cos": cos, "sin": sin,
    }
    ng = norm_g.reshape(1, D_MODEL)
    vt_c, k_c, xbc_c, dt_c = _in_proj_call(ctx, mod_c, ng, weights, consts, latent=False,
                                           tile=min(IN_TILE, ctx.shape[1]))
    vt_l, k_l, xbc_l, dt_l, q, ga, z = _in_proj_call(x, mod_l, ng, weights, consts, latent=True, tile=IN_TILE)

    o_attn = _attn_call(q, k_l, k_c, vt_l, vt_c, tile=ATTN_TILE)

    tok = np.arange(CHUNK)
    lane_head = np.arange(SSD_WIDTH) // SSD_HEAD_DIM
    dt_lane = np.arange(DT_PAD)
    ssd_consts = {
        "conv_w": jnp.pad(conv_w, ((0, 8 - D_CONV), (0, 0))),
        "conv_b": conv_b.reshape(1, CONV_CH),
        "a_log": _pad_lanes(a_log.reshape(1, 2 * SSD_HEADS), DT_PAD),
        "d_skip": jnp.repeat(d_skip, SSD_HEAD_DIM).reshape(1, SSD_WIDTH),
        "shift": jnp.asarray(np.concatenate(
            [np.arange(CONV_WINDOW)[None, :] == (CONV_HALO + tok[:, None] + k - CONV_PAD)
             for k in range(D_CONV) if k != CONV_PAD], axis=0), BF16),
        "tri2": jnp.asarray(np.concatenate([tok[None, :] <= tok[:, None], tok[None, :] >= tok[:, None]], axis=0), BF16),
        "e_fwd": jnp.asarray((dt_lane[:, None] % PIECE_LANES == lane_head[None, :])
                             & (dt_lane[:, None] < 3 * PIECE_LANES), BF16),
        "e_bwd": jnp.asarray((dt_lane[:, None] % PIECE_LANES == lane_head[None, :] + SSD_HEADS)
                             & (dt_lane[:, None] < 3 * PIECE_LANES), BF16),
    }
    y_ssd = _ssd_call(xbc_l, dt_l, xbc_c, dt_c, ssd_consts)

    w_out_bf = w_out.astype(BF16)
    return _out_call(o_attn, ga, y_ssd, z, x, mod_l, w_out_bf[:ATTN_WIDTH], w_out_bf[ATTN_WIDTH:],
                     ssd_norm_g.reshape(1, SSD_WIDTH), final_g.reshape(1, D_MODEL), tile=OUT_TILE)
```

```python
import functools

import jax
import jax.numpy as jnp
import numpy as np
from jax import lax
from jax.experimental import pallas as pl
from jax.experimental.pallas import tpu as pltpu

F32 = jnp.float32
BF16 = jnp.bfloat16

D_MODEL = 1024
GRID_W = 64
HEAD_DIM = 64
N_Q_HEADS = 16
N_KV_HEADS = 4
Q_PER_KV = N_Q_HEADS // N_KV_HEADS
ATTN_WIDTH = N_Q_HEADS * HEAD_DIM
KV_WIDTH = N_KV_HEADS * HEAD_DIM
SSD_HEADS = 16
SSD_HEAD_DIM = 64
SSD_WIDTH = SSD_HEADS * SSD_HEAD_DIM
SSD_GROUPS = 2
GROUP_WIDTH = SSD_WIDTH // SSD_GROUPS
D_STATE = 128
GN = SSD_GROUPS * D_STATE
D_CONV = 5
CONV_PAD = D_CONV // 2
CONV_CH = SSD_WIDTH + 2 * GN
CHUNK = 128
ROPE_THETA = 10000.0
ATTN_SCALE = HEAD_DIM ** -0.5
EPS = 1e-6
OFF_V = KV_WIDTH
OFF_XBC = 2 * KV_WIDTH
OFF_DT = OFF_XBC + CONV_CH
CTX_COLS = OFF_DT + 2 * SSD_HEADS
OFF_Q = CTX_COLS
OFF_GA = OFF_Q + ATTN_WIDTH
OFF_Z = OFF_GA + ATTN_WIDTH

LANES = 128
DT_PAD = LANES
CONV_HALO = 16
CONV_WINDOW = 256
PIECE_LANES = 2 * SSD_HEADS
LOG2E = 1.4426950408889634
VMEM_LIMIT = 56 * 1024 * 1024

IN_TILE = 1024
OUT_TILE = 1024
IN_SUB = 1024
OUT_SUB = 256
ATTN_TILE = 256
REDUCE_PARTS = 8
HEADS_PER_PASS = 2
PREP_UNROLL = 2
SCAN_UNROLL = 2


def _silu(v):
    return v * (1.0 / (1.0 + jnp.exp2(v * (-LOG2E))))


def _softplus(v):
    return jnp.maximum(v, 0.0) + jnp.log1p(jnp.exp(-jnp.abs(v)))


def _split3(v):
    p1 = v.astype(BF16)
    r1 = v - p1.astype(F32)
    p2 = r1.astype(BF16)
    p3 = (r1 - p2.astype(F32)).astype(BF16)
    return p1, p2, p3


def _dot(a, b):
    return jnp.dot(a, b, preferred_element_type=F32)


def _const_spec(shape):
    nd = len(shape)
    return pl.BlockSpec(shape, lambda *_: (0,) * nd, pipeline_mode=pl.Buffered(1))


def _mod_kernel(c_ref, w_ref, b_ref, o_ref):
    s = _silu(c_ref[...])
    o_ref[...] = jnp.dot(s, w_ref[...], preferred_element_type=F32,
                         precision=lax.Precision.HIGHEST) + b_ref[...]


def _mod_call(cc, ada_w, ada_b):
    rows = cc.shape[0]
    n = ada_w.shape[1]
    bn = D_MODEL
    return pl.pallas_call(
        _mod_kernel,
        grid=(n // bn,),
        in_specs=[pl.BlockSpec((rows, D_MODEL), lambda j: (0, 0)),
                  pl.BlockSpec((D_MODEL, bn), lambda j: (0, j)),
                  pl.BlockSpec((1, bn), lambda j: (0, j))],
        out_specs=pl.BlockSpec((rows, bn), lambda j: (0, j)),
        out_shape=jax.ShapeDtypeStruct((rows, n), F32),
        compiler_params=pltpu.CompilerParams(dimension_semantics=("arbitrary",),
                                             vmem_limit_bytes=VMEM_LIMIT),
        name="adaln_mod",
    )(cc, ada_w, ada_b.reshape(1, n))


def _modulated_norm(x, mod, ng):
    ms = jnp.mean(x * x, axis=-1, keepdims=True)
    y = (x * lax.rsqrt(ms + EPS)) * ng
    return y * (1.0 + mod[:, D_MODEL:2 * D_MODEL]) + mod[:, :D_MODEL]


def _head_norm(v, gain, bd):
    ssq = _dot((v * v).astype(BF16), bd)
    return v * lax.rsqrt(ssq * (1.0 / HEAD_DIM) + EPS) * gain


def _rope(v, cos, sin_signed):
    lane = lax.broadcasted_iota(jnp.int32, v.shape, 1)
    up = pltpu.roll(v, LANES - 16, axis=1)
    down = pltpu.roll(v, 16, axis=1)
    partner = jnp.where((lane & 16) == 0, up, down)
    return v * cos + partner * sin_signed


def _in_kernel(x_ref, mod_ref, ng_ref, wk_ref, wv_ref, wx_ref, wdt_ref, dtb_ref, kg_ref, bd_ref, *rest,
               latent, tile):
    if latent:
        wq_ref, wg_ref, wz_ref, qg_ref, cos_ref, sin_ref, vt_out, k_out, xbc_out, dt_out, q_out, ga_out, z_out = rest
        tile_row0 = pl.multiple_of(pl.program_id(1) * tile, tile)
    else:
        vt_out, k_out, xbc_out, dt_out = rest
    sub = min(IN_SUB, tile)

    def normed(c):
        return _modulated_norm(x_ref[0, c * sub:(c + 1) * sub, :], mod_ref[0], ng_ref[...]).astype(BF16)

    def rotated(v, cos, sin):
        if not latent:
            return v
        return jnp.concatenate([_rope(v[:, t * LANES:(t + 1) * LANES], cos, sin)
                                for t in range(KV_WIDTH // LANES)], axis=1)

    def project(c, h):
        r = slice(c * sub, (c + 1) * sub)
        cos = sin = None
        if latent:
            cos = cos_ref[pl.ds(tile_row0 + c * sub, sub), :]
            sin = sin_ref[pl.ds(tile_row0 + c * sub, sub), :]
            for s in range(ATTN_WIDTH // KV_WIDTH):
                cols = slice(s * KV_WIDTH, (s + 1) * KV_WIDTH)
                qn = _head_norm(_dot(h, wq_ref[:, cols]), qg_ref[...], bd_ref[...])
                q_out[0, r, cols] = (rotated(qn, cos, sin) * (ATTN_SCALE * LOG2E)).astype(BF16)
        kn = _head_norm(_dot(h, wk_ref[...]), kg_ref[...], bd_ref[...])
        k_out[0, r, :] = rotated(kn, cos, sin).astype(BF16)
        xbc_out[0, r, :] = _dot(h, wx_ref[...]).astype(BF16)
        if latent:
            ga_out[0, r, :] = _silu(_dot(h, wg_ref[...])).astype(BF16)
            z_out[0, r, :] = _silu(_dot(h, wz_ref[...])).astype(BF16)
        vt_out[0, :, r] = _dot(h, wv_ref[...]).T.astype(BF16)
        dt_raw = _dot(h, wdt_ref[...]) + dtb_ref[...]
        lane = lax.broadcasted_iota(jnp.int32, dt_raw.shape, 1)
        dt_out[0, r, :] = jnp.where(lane < 2 * SSD_HEADS, _softplus(dt_raw), 0.0)

    ready = normed(0)
    for c in range(tile // sub):
        upcoming = normed(c + 1) if (c + 1) * sub < tile else None
        project(c, ready)
        ready = upcoming


def _in_proj_call(x, mod, ng, w, consts, *, latent, tile):
    bsz, rows, _ = x.shape
    nt = rows // tile
    tok = lambda width: pl.BlockSpec((1, tile, width), lambda b, j: (b, j, 0))
    mod_map = (lambda b, j: (b, 0, 0)) if latent else (lambda b, j: (0, 0, 0))
    in_specs = [tok(D_MODEL),
                pl.BlockSpec((1, 1, 3 * D_MODEL), mod_map),
                _const_spec((1, D_MODEL)),
                _const_spec((D_MODEL, KV_WIDTH)), _const_spec((D_MODEL, KV_WIDTH)),
                _const_spec((D_MODEL, CONV_CH)), _const_spec((D_MODEL, DT_PAD)),
                _const_spec((1, DT_PAD)), _const_spec((1, KV_WIDTH)), _const_spec((KV_WIDTH, KV_WIDTH))]
    args = [x, mod, ng, w["k"], w["v"], w["xbc"], w["dt"], consts["dt_bias"], consts["k_gain"], consts["bd"]]
    out_specs = [pl.BlockSpec((1, KV_WIDTH, tile), lambda b, j: (b, 0, j)),
                 tok(KV_WIDTH), tok(CONV_CH), tok(DT_PAD)]
    out_shape = [jax.ShapeDtypeStruct((bsz, KV_WIDTH, rows), BF16),
                 jax.ShapeDtypeStruct((bsz, rows, KV_WIDTH), BF16),
                 jax.ShapeDtypeStruct((bsz, rows, CONV_CH), BF16),
                 jax.ShapeDtypeStruct((bsz, rows, DT_PAD), F32)]
    if latent:
        in_specs += [_const_spec((D_MODEL, ATTN_WIDTH)), _const_spec((D_MODEL, ATTN_WIDTH)),
                     _const_spec((D_MODEL, SSD_WIDTH)), _const_spec((1, KV_WIDTH)),
                     _const_spec((rows, LANES)), _const_spec((rows, LANES))]
        args += [w["q"], w["ga"], w["z"], consts["q_gain"], consts["cos"], consts["sin"]]
        out_specs += [tok(ATTN_WIDTH), tok(ATTN_WIDTH), tok(SSD_WIDTH)]
        out_shape += [jax.ShapeDtypeStruct((bsz, rows, ATTN_WIDTH), BF16),
                      jax.ShapeDtypeStruct((bsz, rows, ATTN_WIDTH), BF16),
                      jax.ShapeDtypeStruct((bsz, rows, SSD_WIDTH), BF16)]
    body = functools.partial(_in_kernel, latent=latent, tile=tile)
    return pl.pallas_call(
        body,
        grid=(bsz, nt),
        in_specs=in_specs,
        out_specs=out_specs,
        out_shape=out_shape,
        compiler_params=pltpu.CompilerParams(dimension_semantics=("arbitrary", "arbitrary"),
                                             vmem_limit_bytes=VMEM_LIMIT),
        name="in_proj_latent" if latent else "in_proj_context",
    )(*args)


def _split_reduce(op, a):
    parts = a.reshape(REDUCE_PARTS, a.shape[0] // REDUCE_PARTS, a.shape[1])
    return op(op(parts, axis=1), axis=0, keepdims=True)


def _attn_kernel(q_ref, kl_ref, kc_ref, vtl_ref, vtc_ref, o_ref, s_scr, p_scr):
    tile = q_ref.shape[1]
    rows = kl_ref.shape[1]
    q_t = q_ref[0].T
    k_l = kl_ref[0]
    k_c = kc_ref[0]
    zeros = jnp.zeros((HEAD_DIM, tile), BF16)
    n_pass = N_Q_HEADS // HEADS_PER_PASS

    def scores(pass_idx):
        kv = pass_idx * HEADS_PER_PASS // Q_PER_KV
        q_pad = jnp.concatenate(
            [jnp.concatenate([zeros] * kv + [q_t[hq * HEAD_DIM:(hq + 1) * HEAD_DIM, :]]
                             + [zeros] * (N_KV_HEADS - 1 - kv), axis=0)
             for hq in range(pass_idx * HEADS_PER_PASS, (pass_idx + 1) * HEADS_PER_PASS)], axis=1)
        s_scr[pass_idx % 2, :rows, :] = _dot(k_l, q_pad)
        s_scr[pass_idx % 2, rows:, :] = _dot(k_c, q_pad)

    def finish(pass_idx):
        slot = pass_idx % 2
        kv = pass_idx * HEADS_PER_PASS // Q_PER_KV
        kv_rows = slice(kv * HEAD_DIM, (kv + 1) * HEAD_DIM)
        m = _split_reduce(jnp.max, s_scr[slot])
        p = jnp.exp2(s_scr[slot] - m)
        denom = _split_reduce(jnp.sum, p)
        p_scr[slot] = p.astype(BF16)
        o_t = _dot(vtl_ref[0, kv_rows, :], p_scr[slot, :rows, :]) + _dot(vtc_ref[0, kv_rows, :], p_scr[slot, rows:, :])
        o_t = o_t * (1.0 / denom)
        return [o_t[:, t * tile:(t + 1) * tile] for t in range(HEADS_PER_PASS)]

    outs = []
    scores(0)
    for pass_idx in range(n_pass):
        if pass_idx + 1 < n_pass:
            scores(pass_idx + 1)
        outs += finish(pass_idx)
    o_ref[0] = jnp.concatenate(outs, axis=0).T.astype(BF16)


def _attn_call(q, k_l, k_c, vt_l, vt_c, *, tile):
    bsz, rows, _ = q.shape
    lc = k_c.shape[1]
    per_batch = lambda shape: pl.BlockSpec((1,) + shape, lambda b, j: (b, 0, 0))
    return pl.pallas_call(
        _attn_kernel,
        grid=(bsz, rows // tile),
        in_specs=[pl.BlockSpec((1, tile, ATTN_WIDTH), lambda b, j: (b, j, 0)),
                  per_batch((rows, KV_WIDTH)), per_batch((lc, KV_WIDTH)),
                  per_batch((KV_WIDTH, rows)), per_batch((KV_WIDTH, lc))],
        out_specs=pl.BlockSpec((1, tile, ATTN_WIDTH), lambda b, j: (b, j, 0)),
        out_shape=jax.ShapeDtypeStruct((bsz, rows, ATTN_WIDTH), BF16),
        scratch_shapes=[pltpu.VMEM((2, rows + lc, HEADS_PER_PASS * tile), F32),
                        pltpu.VMEM((2, rows + lc, HEADS_PER_PASS * tile), BF16)],
        compiler_params=pltpu.CompilerParams(dimension_semantics=("arbitrary", "arbitrary"),
                                             vmem_limit_bytes=VMEM_LIMIT),
        name="gqa_attention",
    )(q, k_l, k_c, vt_l, vt_c)


def _conv_chunk(src_ref, dst_ref, cw_ref, cb_ref, shift_ref, i, n_chunks):
    i = jnp.asarray(i, jnp.int32)
    row0 = pl.multiple_of(i * CHUNK, CHUNK)
    main = src_ref[0, pl.ds(row0, CHUNK), :]
    prev0 = pl.multiple_of(jnp.maximum(row0 - CONV_HALO, 0), CONV_HALO)
    next0 = pl.multiple_of(jnp.minimum(row0 + CHUNK, (n_chunks - 1) * CHUNK + CHUNK - CONV_HALO), CONV_HALO)
    prev = src_ref[0, pl.ds(prev0, CONV_HALO), :] * (i > 0).astype(BF16)
    nxt = src_ref[0, pl.ds(next0, CONV_HALO), :] * (i < n_chunks - 1).astype(BF16)
    fill = jnp.zeros((CONV_WINDOW - CHUNK - 2 * CONV_HALO, CONV_CH), BF16)
    win = jnp.concatenate([prev, main, nxt, fill], axis=0)
    off_taps = [k for k in range(D_CONV) if k != CONV_PAD]
    for c0 in range(0, CONV_CH, GROUP_WIDTH):
        cols = slice(c0, c0 + GROUP_WIDTH)
        shifted = _dot(shift_ref[...], win[:, cols])
        acc = cb_ref[:, cols] + cw_ref[CONV_PAD:CONV_PAD + 1, cols] * main[:, cols].astype(F32)
        for j, k in enumerate(off_taps):
            acc = acc + cw_ref[k:k + 1, cols] * shifted[j * CHUNK:(j + 1) * CHUNK, :]
        dst_ref[pl.ds(row0, CHUNK), cols] = _silu(acc).astype(BF16)


def _pack3(v):
    p1, p2, p3 = _split3(v)
    packed = p1.astype(F32) + pltpu.roll(p2.astype(F32), PIECE_LANES, axis=1) \
        + pltpu.roll(p3.astype(F32), 2 * PIECE_LANES, axis=1)
    return packed.astype(BF16)


def _chunk_terms(dt_ref, terms, i, aneg2, tri2, lane_ok, lane_fwd, piece_fwd):
    acs_ref, pk_dt_ref, pk_end_ref, pk_start_ref = terms
    rows = pl.ds(pl.multiple_of(jnp.asarray(i, jnp.int32) * CHUNK, CHUNK), CHUNK)
    dtc = dt_ref[0, rows, :]
    r = _dot(tri2, _pack3(dtc * aneg2))
    r = jnp.where(piece_fwd, r[:CHUNK], r[CHUNK:])
    a_cs = (r + pltpu.roll(r, LANES - PIECE_LANES, axis=1) + pltpu.roll(r, LANES - 2 * PIECE_LANES, axis=1)) * lane_ok
    total = jnp.where(lane_fwd, a_cs[CHUNK - 1:CHUNK, :], a_cs[0:1, :])
    acs_ref[rows, :] = a_cs
    pk_dt_ref[rows, :] = _pack3(dtc)
    pk_end_ref[rows, :] = _pack3(dtc * jnp.exp2(total - a_cs) * lane_ok)
    pk_start_ref[rows, :] = _pack3(jnp.exp2(a_cs) * lane_ok)


def _chunk_cb(act_ref, cb_ref, i):
    rows = pl.ds(pl.multiple_of(jnp.asarray(i, jnp.int32) * CHUNK, CHUNK), CHUNK)
    for g in range(SSD_GROUPS):
        bg = act_ref[rows, SSD_WIDTH + g * D_STATE:SSD_WIDTH + (g + 1) * D_STATE]
        cg = act_ref[rows, SSD_WIDTH + GN + g * D_STATE:SSD_WIDTH + GN + (g + 1) * D_STATE]
        cb_ref[rows, g * CHUNK:(g + 1) * CHUNK] = lax.dot_general(
            cg, bg, (((1,), (1,)), ((), ())), preferred_element_type=F32)


def _ssd_chunk(act_ref, terms, cb_ref, row0, direction, want_y, expand, state_ref):
    acs_ref, pk_dt_ref, pk_end_ref, pk_start_ref = terms
    rows = pl.ds(row0, CHUNK)
    last = CHUNK - 1 if direction == 0 else 0
    a_cs = acs_ref[rows, :]
    dt_end_b = _dot(pk_end_ref[rows, :], expand)
    start_b = _dot(pk_start_ref[rows, :], expand)
    chunk_decay_b = start_b[last:last + 1, :]
    xs = act_ref[rows, :SSD_WIDTH].astype(F32)
    bm = act_ref[rows, SSD_WIDTH:SSD_WIDTH + GN]
    x_end = (xs * dt_end_b).astype(BF16)
    state = state_ref[...]
    made = [lax.dot_general(bm[:, g * D_STATE:(g + 1) * D_STATE],
                            x_end[:, g * GROUP_WIDTH:(g + 1) * GROUP_WIDTH],
                            (((0,), (0,)), ((), ())), preferred_element_type=F32)
            for g in range(SSD_GROUPS)]
    state_ref[...] = state * chunk_decay_b + jnp.concatenate(made, axis=1)
    if not want_y:
        return None
    cm = act_ref[rows, SSD_WIDTH + GN:]
    x_bf = (xs * _dot(pk_dt_ref[rows, :], expand)).astype(BF16)
    state_bf = state.astype(BF16)
    a_cs_t = a_cs.T
    li = lax.broadcasted_iota(jnp.int32, (CHUNK, CHUNK), 0)
    si = lax.broadcasted_iota(jnp.int32, (CHUNK, CHUNK), 1)
    visible = (si <= li) if direction == 0 else (si >= li)
    slab_lane = lax.broadcasted_iota(jnp.int32, (CHUNK, KV_WIDTH), 1)
    heads_per_group = SSD_HEADS // SSD_GROUPS
    heads_per_slab = KV_WIDTH // SSD_HEAD_DIM
    slabs = []
    for g in range(SSD_GROUPS):
        cg = cm[:, g * D_STATE:(g + 1) * D_STATE]
        gcols = slice(g * GROUP_WIDTH, (g + 1) * GROUP_WIDTH)
        y_off = _dot(cg, state_bf[:, gcols]) * start_b[:, gcols]
        cb = cb_ref[rows, g * CHUNK:(g + 1) * CHUNK]
        for sl in range(GROUP_WIDTH // KV_WIDTH):
            slab0 = g * GROUP_WIDTH + sl * KV_WIDTH
            x_slab = x_bf[:, slab0:slab0 + KV_WIDTH]
            acc = y_off[:, sl * KV_WIDTH:(sl + 1) * KV_WIDTH]
            for pair in range(heads_per_slab // 2):
                mats, xheads = [], []
                for t in (2 * pair, 2 * pair + 1):
                    hd = g * heads_per_group + sl * heads_per_slab + t
                    lane = direction * SSD_HEADS + hd
                    seg = a_cs[:, lane:lane + 1] - a_cs_t[lane:lane + 1, :]
                    lmat = jnp.exp2(jnp.where(visible, seg, -jnp.inf))
                    mats.append((cb * lmat).astype(BF16))
                    in_head = (slab_lane >= t * SSD_HEAD_DIM) & (slab_lane < (t + 1) * SSD_HEAD_DIM)
                    xheads.append(jnp.where(in_head, x_slab, jnp.zeros_like(x_slab)))
                acc = acc + _dot(jnp.concatenate(mats, axis=1), jnp.concatenate(xheads, axis=0))
            slabs.append(acc)
    return jnp.concatenate(slabs, axis=1)


def _ssd_kernel(xl_ref, dtl_ref, xc_ref, dtc_ref, cw_ref, cb_ref, alog_ref, dskip_ref, shift_ref, tri2_ref,
                e_fwd_ref, e_bwd_ref, y_ref, act_l, act_c, y_part, state_f, state_b,
                acs_l, pkdt_l, pkend_l, pkstart_l, acs_c, pkdt_c, pkend_c, pkstart_c, cbt_l, *, n_lat, n_ctx):
    aneg2 = -jnp.exp(alog_ref[...]) * LOG2E
    dt_lane = lax.broadcasted_iota(jnp.int32, (1, DT_PAD), 1)
    lane_ok = (dt_lane < 2 * SSD_HEADS).astype(F32)
    lane_fwd = dt_lane < SSD_HEADS
    piece_fwd = (dt_lane % PIECE_LANES) < SSD_HEADS
    terms_l = (acs_l, pkdt_l, pkend_l, pkstart_l)
    terms_c = (acs_c, pkdt_c, pkend_c, pkstart_c)

    def prepare(src_ref, dt_ref, act_ref, terms, cbt_ref, i, n_chunks):
        _conv_chunk(src_ref, act_ref, cw_ref, cb_ref, shift_ref, i, n_chunks)
        _chunk_terms(dt_ref, terms, i, aneg2, tri2_ref[...], lane_ok, lane_fwd, piece_fwd)
        if cbt_ref is not None:
            _chunk_cb(act_ref, cbt_ref, i)

    lax.fori_loop(0, n_lat, lambda i, c: (prepare(xl_ref, dtl_ref, act_l, terms_l, cbt_l, i, n_lat), c)[1], 0,
                  unroll=PREP_UNROLL)
    for i in range(n_ctx):
        prepare(xc_ref, dtc_ref, act_c, terms_c, None, i, n_ctx)
    expands = (e_fwd_ref[...], e_bwd_ref[...])
    states = (state_f, state_b)
    for direction in range(2):
        states[direction][...] = jnp.zeros_like(states[direction])
        for i in range(n_ctx):
            chunk = i if direction == 0 else n_ctx - 1 - i
            _ssd_chunk(act_c, terms_c, None, chunk * CHUNK, direction, False, expands[direction], states[direction])

    def step(i, second_half):
        for direction in range(2):
            chunk = i if direction == 0 else n_lat - 1 - i
            row0 = pl.multiple_of(chunk * CHUNK, CHUNK)
            y = _ssd_chunk(act_l, terms_l, cbt_l, row0, direction, True, expands[direction], states[direction])
            rows = pl.ds(row0, CHUNK)
            if second_half:
                xs = act_l[rows, :SSD_WIDTH].astype(F32)
                y_ref[0, rows, :] = (y_part[rows, :] + y + dskip_ref[...] * xs).astype(BF16)
            else:
                y_part[rows, :] = y

    half = n_lat // 2
    lax.fori_loop(0, half, lambda i, c: (step(i, False), c)[1], 0, unroll=SCAN_UNROLL)
    lax.fori_loop(half, n_lat, lambda i, c: (step(i, True), c)[1], 0, unroll=SCAN_UNROLL)


def _ssd_call(xbc_l, dt_l, xbc_c, dt_c, consts):
    bsz, rows, _ = xbc_l.shape
    lc = xbc_c.shape[1]
    per_batch = lambda shape: pl.BlockSpec((1,) + shape, lambda b: (b, 0, 0))
    assert rows % (2 * CHUNK) == 0 and lc % CHUNK == 0
    body = functools.partial(_ssd_kernel, n_lat=rows // CHUNK, n_ctx=lc // CHUNK)
    return pl.pallas_call(
        body,
        grid=(bsz,),
        in_specs=[per_batch((rows, CONV_CH)), per_batch((rows, DT_PAD)),
                  per_batch((lc, CONV_CH)), per_batch((lc, DT_PAD)),
                  _const_spec((8, CONV_CH)), _const_spec((1, CONV_CH)), _const_spec((1, DT_PAD)),
                  _const_spec((1, SSD_WIDTH)), _const_spec(((D_CONV - 1) * CHUNK, CONV_WINDOW)),
                  _const_spec((2 * CHUNK, CHUNK)),
                  _const_spec((DT_PAD, SSD_WIDTH)), _const_spec((DT_PAD, SSD_WIDTH))],
        out_specs=per_batch((rows, SSD_WIDTH)),
        out_shape=jax.ShapeDtypeStruct((bsz, rows, SSD_WIDTH), BF16),
        scratch_shapes=[pltpu.VMEM((rows, CONV_CH), BF16), pltpu.VMEM((lc, CONV_CH), BF16),
                        pltpu.VMEM((rows, SSD_WIDTH), F32),
                        pltpu.VMEM((D_STATE, SSD_WIDTH), F32), pltpu.VMEM((D_STATE, SSD_WIDTH), F32)]
                       + [pltpu.VMEM((n, DT_PAD), dt) for n in (rows, lc) for dt in (F32, BF16, BF16, BF16)]
                       + [pltpu.VMEM((rows, SSD_GROUPS * CHUNK), F32)],
        compiler_params=pltpu.CompilerParams(dimension_semantics=("arbitrary",),
                                             vmem_limit_bytes=VMEM_LIMIT),
        name="ssd_bidir",
    )(xbc_l, dt_l, xbc_c, dt_c, consts["conv_w"], consts["conv_b"], consts["a_log"], consts["d_skip"],
      consts["shift"], consts["tri2"], consts["e_fwd"], consts["e_bwd"])


def _out_kernel(o_ref, ga_ref, y_ref, z_ref, x_ref, mod_ref, wa_ref, ws_ref, sg_ref, fg_ref, out_ref):
    n_sub = o_ref.shape[1] // OUT_SUB
    gate = mod_ref[0][:, 2 * D_MODEL:]

    def gated(c):
        r = slice(c * OUT_SUB, (c + 1) * OUT_SUB)
        y_a = o_ref[0, r, :] * ga_ref[0, r, :]
        t = y_ref[0, r, :].astype(F32) * z_ref[0, r, :].astype(F32)
        y_s = (t * lax.rsqrt(jnp.mean(t * t, axis=-1, keepdims=True) + EPS) * sg_ref[...]).astype(BF16)
        return y_a, y_s

    def project(c, y_a, y_s):
        r = slice(c * OUT_SUB, (c + 1) * OUT_SUB)
        new = x_ref[0, r, :] + gate * (_dot(y_a, wa_ref[...]) + _dot(y_s, ws_ref[...]))
        out_ref[0, r, :] = new * lax.rsqrt(jnp.mean(new * new, axis=-1, keepdims=True) + EPS) * fg_ref[...]

    ready = gated(0)
    for c in range(n_sub):
        upcoming = gated(c + 1) if c + 1 < n_sub else None
        project(c, *ready)
        ready = upcoming


def _out_call(o_attn, ga, y_ssd, z, x, mod, w_a, w_s, ssd_gain, final_gain, *, tile):
    bsz, rows, _ = x.shape
    tok = pl.BlockSpec((1, tile, D_MODEL), lambda b, j: (b, j, 0))
    return pl.pallas_call(
        _out_kernel,
        grid=(bsz, rows // tile),
        in_specs=[tok, tok, tok, tok, tok,
                  pl.BlockSpec((1, 1, 3 * D_MODEL), lambda b, j: (b, 0, 0)),
                  _const_spec((ATTN_WIDTH, D_MODEL)), _const_spec((SSD_WIDTH, D_MODEL)),
                  _const_spec((1, SSD_WIDTH)), _const_spec((1, D_MODEL))],
        out_specs=tok,
        out_shape=jax.ShapeDtypeStruct((bsz, rows, D_MODEL), F32),
        compiler_params=pltpu.CompilerParams(dimension_semantics=("arbitrary", "arbitrary"),
                                             vmem_limit_bytes=VMEM_LIMIT),
        name="merge_out_proj",
    )(o_attn, ga, y_ssd, z, x, mod, w_a, w_s, ssd_gain, final_gain)


def _rope_tables(rows):
    n_freq = HEAD_DIM // 4
    t = jnp.arange(rows, dtype=jnp.int32)
    pos = jnp.stack([(t // GRID_W).astype(F32), (t % GRID_W).astype(F32)], axis=1)
    inv_freq = ROPE_THETA ** (-jnp.arange(n_freq, dtype=F32) / n_freq)
    ang = pos[:, :, None] * inv_freq
    cos = jnp.cos(ang)[:, :, None, :]
    sin = jnp.sin(ang)[:, :, None, :]
    cos_h = jnp.broadcast_to(cos, (rows, 2, 2, n_freq)).reshape(rows, HEAD_DIM)
    sin_h = jnp.concatenate([-sin, sin], axis=2).reshape(rows, HEAD_DIM)
    reps = LANES // HEAD_DIM
    return jnp.tile(cos_h, (1, reps)), jnp.tile(sin_h, (1, reps))


def _pad_lanes(v, width):
    return jnp.pad(v, [(0, 0)] * (v.ndim - 1) + [(0, width - v.shape[-1])])


def kernel(x, c, ctx, c_ctx, ada_w, ada_b, norm_g, w_in, conv_w, conv_b, dt_bias, a_log, d_skip,
           q_norm_g, k_norm_g, ssd_norm_g, w_out, final_g):
    assert ada_w.shape[0] == 1, "single-layer problem: context outputs are never needed"
    bsz, rows, _ = x.shape
    ada_w, ada_b, norm_g, w_in, conv_w, conv_b = ada_w[0], ada_b[0], norm_g[0], w_in[0], conv_w[0], conv_b[0]
    dt_bias, a_log, d_skip = dt_bias[0], a_log[0], d_skip[0]
    q_norm_g, k_norm_g, ssd_norm_g, w_out = q_norm_g[0], k_norm_g[0], ssd_norm_g[0], w_out[0]

    n_rows = -(-(bsz + 1) // 8) * 8
    cc = jnp.concatenate([c, c_ctx[None, :], jnp.zeros((n_rows - bsz - 1, D_MODEL), F32)], axis=0)
    mod = _mod_call(cc, ada_w, ada_b)
    mod_l = mod[:bsz].reshape(bsz, 1, 3 * D_MODEL)
    mod_c = mod[bsz:bsz + 1].reshape(1, 1, 3 * D_MODEL)

    w_bf = w_in.astype(BF16)
    weights = {
        "k": w_bf[:, :OFF_V], "v": w_bf[:, OFF_V:OFF_XBC], "xbc": w_bf[:, OFF_XBC:OFF_DT],
        "dt": _pad_lanes(w_bf[:, OFF_DT:CTX_COLS], DT_PAD),
        "q": w_bf[:, OFF_Q:OFF_GA], "ga": w_bf[:, OFF_GA:OFF_Z], "z": w_bf[:, OFF_Z:],
    }
    cos, sin = _rope_tables(rows)
    head_of_lane = np.arange(KV_WIDTH) // HEAD_DIM
    consts = {
        "dt_bias": _pad_lanes(dt_bias.reshape(1, 2 * SSD_HEADS), DT_PAD),
        "k_gain": jnp.tile(k_norm_g, KV_WIDTH // HEAD_DIM).reshape(1, KV_WIDTH),
        "q_gain": jnp.tile(q_norm_g, KV_WIDTH // HEAD_DIM).reshape(1, KV_WIDTH),
        "bd": jnp.asarray(head_of_lane[:, None] == head_of_lane[None, :], BF16),
        "cos": cos, "sin": sin,
    }
    ng = norm_g.reshape(1, D_MODEL)
    vt_c, k_c, xbc_c, dt_c = _in_proj_call(ctx, mod_c, ng, weights, consts, latent=False,
                                           tile=min(IN_TILE, ctx.shape[1]))
    vt_l, k_l, xbc_l, dt_l, q, ga, z = _in_proj_call(x, mod_l, ng, weights, consts, latent=True, tile=IN_TILE)

    o_attn = _attn_call(q, k_l, k_c, vt_l, vt_c, tile=ATTN_TILE)

    tok = np.arange(CHUNK)
    lane_head = np.arange(SSD_WIDTH) // SSD_HEAD_DIM
    dt_lane = np.arange(DT_PAD)
    ssd_consts = {
        "conv_w": jnp.pad(conv_w, ((0, 8 - D_CONV), (0, 0))),
        "conv_b": conv_b.reshape(1, CONV_CH),
        "a_log": _pad_lanes(a_log.reshape(1, 2 * SSD_HEADS), DT_PAD),
        "d_skip": jnp.repeat(d_skip, SSD_HEAD_DIM).reshape(1, SSD_WIDTH),
        "shift": jnp.asarray(np.concatenate(
            [np.arange(CONV_WINDOW)[None, :] == (CONV_HALO + tok[:, None] + k - CONV_PAD)
             for k in range(D_CONV) if k != CONV_PAD], axis=0), BF16),
        "tri2": jnp.asarray(np.concatenate([tok[None, :] <= tok[:, None], tok[None, :] >= tok[:, None]], axis=0), BF16),
        "e_fwd": jnp.asarray((dt_lane[:, None] % PIECE_LANES == lane_head[None, :])
                             & (dt_lane[:, None] < 3 * PIECE_LANES), BF16),
        "e_bwd": jnp.asarray((dt_lane[:, None] % PIECE_LANES == lane_head[None, :] + SSD_HEADS)
                             & (dt_lane[:, None] < 3 * PIECE_LANES), BF16),
    }
    y_ssd = _ssd_call(xbc_l, dt_l, xbc_c, dt_c, ssd_consts)

    w_out_bf = w_out.astype(BF16)
    return _out_call(o_attn, ga, y_ssd, z, x, mod_l, w_out_bf[:ATTN_WIDTH], w_out_bf[ATTN_WIDTH:],
                     ssd_norm_g.reshape(1, SSD_WIDTH), final_g.reshape(1, D_MODEL), tile=OUT_TILE)
```

```python
import functools

import jax
import jax.numpy as jnp
import numpy as np
from jax import lax
from jax.experimental import pallas as pl
from jax.experimental.pallas import tpu as pltpu

F32 = jnp.float32
BF16 = jnp.bfloat16

D_MODEL = 1024
GRID_W = 64
HEAD_DIM = 64
N_Q_HEADS = 16
N_KV_HEADS = 4
Q_PER_KV = N_Q_HEADS // N_KV_HEADS
ATTN_WIDTH = N_Q_HEADS * HEAD_DIM
KV_WIDTH = N_KV_HEADS * HEAD_DIM
SSD_HEADS = 16
SSD_HEAD_DIM = 64
SSD_WIDTH = SSD_HEADS * SSD_HEAD_DIM
SSD_GROUPS = 2
GROUP_WIDTH = SSD_WIDTH // SSD_GROUPS
D_STATE = 128
GN = SSD_GROUPS * D_STATE
D_CONV = 5
CONV_PAD = D_CONV // 2
CONV_CH = SSD_WIDTH + 2 * GN
CHUNK = 128
ROPE_THETA = 10000.0
ATTN_SCALE = HEAD_DIM ** -0.5
EPS = 1e-6
OFF_V = KV_WIDTH
OFF_XBC = 2 * KV_WIDTH
OFF_DT = OFF_XBC + CONV_CH
CTX_COLS = OFF_DT + 2 * SSD_HEADS
OFF_Q = CTX_COLS
OFF_GA = OFF_Q + ATTN_WIDTH
OFF_Z = OFF_GA + ATTN_WIDTH

LANES = 128
DT_PAD = LANES
CONV_HALO = 16
CONV_WINDOW = 256
PIECE_LANES = 2 * SSD_HEADS
LOG2E = 1.4426950408889634
VMEM_LIMIT = 56 * 1024 * 1024

IN_TILE = 1024
OUT_TILE = 1024
IN_SUB = 1024
OUT_SUB = 256
ATTN_TILE = 256
KEY_BLOCK = 256
LOOKAHEAD = 3
HEADS_PER_PASS = 2
PREP_UNROLL = 4
SCAN_UNROLL = 2


def _silu(v):
    return v * (1.0 / (1.0 + jnp.exp2(v * (-LOG2E))))


def _softplus(v):
    return jnp.maximum(v, 0.0) + jnp.log1p(jnp.exp(-jnp.abs(v)))


def _split3(v):
    p1 = v.astype(BF16)
    r1 = v - p1.astype(F32)
    p2 = r1.astype(BF16)
    p3 = (r1 - p2.astype(F32)).astype(BF16)
    return p1, p2, p3


def _dot(a, b):
    return jnp.dot(a, b, preferred_element_type=F32)


def _const_spec(shape):
    nd = len(shape)
    return pl.BlockSpec(shape, lambda *_: (0,) * nd, pipeline_mode=pl.Buffered(1))


def _mod_kernel(c_ref, w_ref, b_ref, o_ref):
    s = _silu(c_ref[...])
    o_ref[...] = jnp.dot(s, w_ref[...], preferred_element_type=F32,
                         precision=lax.Precision.HIGHEST) + b_ref[...]


def _mod_call(cc, ada_w, ada_b):
    rows = cc.shape[0]
    n = ada_w.shape[1]
    bn = D_MODEL
    return pl.pallas_call(
        _mod_kernel,
        grid=(n // bn,),
        in_specs=[pl.BlockSpec((rows, D_MODEL), lambda j: (0, 0)),
                  pl.BlockSpec((D_MODEL, bn), lambda j: (0, j)),
                  pl.BlockSpec((1, bn), lambda j: (0, j))],
        out_specs=pl.BlockSpec((rows, bn), lambda j: (0, j)),
        out_shape=jax.ShapeDtypeStruct((rows, n), F32),
        compiler_params=pltpu.CompilerParams(dimension_semantics=("arbitrary",),
                                             vmem_limit_bytes=VMEM_LIMIT),
        name="adaln_mod",
    )(cc, ada_w, ada_b.reshape(1, n))


def _modulated_norm(x, mod, ng):
    ms = jnp.mean(x * x, axis=-1, keepdims=True)
    y = (x * lax.rsqrt(ms + EPS)) * ng
    return y * (1.0 + mod[:, D_MODEL:2 * D_MODEL]) + mod[:, :D_MODEL]


def _head_norm(v, gain, bd):
    ssq = _dot((v * v).astype(BF16), bd)
    return v * lax.rsqrt(ssq * (1.0 / HEAD_DIM) + EPS) * gain


def _rope(v, cos, sin_signed):
    lane = lax.broadcasted_iota(jnp.int32, v.shape, 1)
    up = pltpu.roll(v, LANES - 16, axis=1)
    down = pltpu.roll(v, 16, axis=1)
    partner = jnp.where((lane & 16) == 0, up, down)
    return v * cos + partner * sin_signed


def _in_kernel(x_ref, mod_ref, ng_ref, wk_ref, wv_ref, wx_ref, wdt_ref, dtb_ref, kg_ref, bd_ref, *rest,
               latent, tile):
    if latent:
        wq_ref, wg_ref, wz_ref, qg_ref, cos_ref, sin_ref, vt_out, k_out, xbc_out, dt_out, q_out, ga_out, z_out = rest
        tile_row0 = pl.multiple_of(pl.program_id(1) * tile, tile)
    else:
        vt_out, k_out, xbc_out, dt_out = rest
    sub = min(IN_SUB, tile)

    def normed(c):
        return _modulated_norm(x_ref[0, c * sub:(c + 1) * sub, :], mod_ref[0], ng_ref[...]).astype(BF16)

    def rotated(v, cos, sin):
        if not latent:
            return v
        return jnp.concatenate([_rope(v[:, t * LANES:(t + 1) * LANES], cos, sin)
                                for t in range(KV_WIDTH // LANES)], axis=1)

    def project(c, h):
        r = slice(c * sub, (c + 1) * sub)
        cos = sin = None
        if latent:
            cos = cos_ref[pl.ds(tile_row0 + c * sub, sub), :]
            sin = sin_ref[pl.ds(tile_row0 + c * sub, sub), :]
            for s in range(ATTN_WIDTH // KV_WIDTH):
                cols = slice(s * KV_WIDTH, (s + 1) * KV_WIDTH)
                qn = _head_norm(_dot(h, wq_ref[:, cols]), qg_ref[...], bd_ref[...])
                q_out[0, r, cols] = (rotated(qn, cos, sin) * (ATTN_SCALE * LOG2E)).astype(BF16)
        kn = _head_norm(_dot(h, wk_ref[...]), kg_ref[...], bd_ref[...])
        k_out[0, r, :] = rotated(kn, cos, sin).astype(BF16)
        xbc_out[0, r, :] = _dot(h, wx_ref[...]).astype(BF16)
        if latent:
            ga_out[0, r, :] = _silu(_dot(h, wg_ref[...])).astype(BF16)
            z_out[0, r, :] = _silu(_dot(h, wz_ref[...])).astype(BF16)
        vt_out[0, :, r] = _dot(h, wv_ref[...]).T.astype(BF16)
        dt_raw = _dot(h, wdt_ref[...]) + dtb_ref[...]
        lane = lax.broadcasted_iota(jnp.int32, dt_raw.shape, 1)
        dt_out[0, r, :] = jnp.where(lane < 2 * SSD_HEADS, _softplus(dt_raw), 0.0)

    ready = normed(0)
    for c in range(tile // sub):
        upcoming = normed(c + 1) if (c + 1) * sub < tile else None
        project(c, ready)
        ready = upcoming


def _in_proj_call(x, mod, ng, w, consts, *, latent, tile):
    bsz, rows, _ = x.shape
    nt = rows // tile
    tok = lambda width: pl.BlockSpec((1, tile, width), lambda b, j: (b, j, 0))
    mod_map = (lambda b, j: (b, 0, 0)) if latent else (lambda b, j: (0, 0, 0))
    in_specs = [tok(D_MODEL),
                pl.BlockSpec((1, 1, 3 * D_MODEL), mod_map),
                _const_spec((1, D_MODEL)),
                _const_spec((D_MODEL, KV_WIDTH)), _const_spec((D_MODEL, KV_WIDTH)),
                _const_spec((D_MODEL, CONV_CH)), _const_spec((D_MODEL, DT_PAD)),
                _const_spec((1, DT_PAD)), _const_spec((1, KV_WIDTH)), _const_spec((KV_WIDTH, KV_WIDTH))]
    args = [x, mod, ng, w["k"], w["v"], w["xbc"], w["dt"], consts["dt_bias"], consts["k_gain"], consts["bd"]]
    out_specs = [pl.BlockSpec((1, KV_WIDTH, tile), lambda b, j: (b, 0, j)),
                 tok(KV_WIDTH), tok(CONV_CH), tok(DT_PAD)]
    out_shape = [jax.ShapeDtypeStruct((bsz, KV_WIDTH, rows), BF16),
                 jax.ShapeDtypeStruct((bsz, rows, KV_WIDTH), BF16),
                 jax.ShapeDtypeStruct((bsz, rows, CONV_CH), BF16),
                 jax.ShapeDtypeStruct((bsz, rows, DT_PAD), F32)]
    if latent:
        in_specs += [_const_spec((D_MODEL, ATTN_WIDTH)), _const_spec((D_MODEL, ATTN_WIDTH)),
                     _const_spec((D_MODEL, SSD_WIDTH)), _const_spec((1, KV_WIDTH)),
                     _const_spec((rows, LANES)), _const_spec((rows, LANES))]
        args += [w["q"], w["ga"], w["z"], consts["q_gain"], consts["cos"], consts["sin"]]
        out_specs += [tok(ATTN_WIDTH), tok(ATTN_WIDTH), tok(SSD_WIDTH)]
        out_shape += [jax.ShapeDtypeStruct((bsz, rows, ATTN_WIDTH), BF16),
                      jax.ShapeDtypeStruct((bsz, rows, ATTN_WIDTH), BF16),
                      jax.ShapeDtypeStruct((bsz, rows, SSD_WIDTH), BF16)]
    body = functools.partial(_in_kernel, latent=latent, tile=tile)
    return pl.pallas_call(
        body,
        grid=(bsz, nt),
        in_specs=in_specs,
        out_specs=out_specs,
        out_shape=out_shape,
        compiler_params=pltpu.CompilerParams(dimension_semantics=("arbitrary", "arbitrary"),
                                             vmem_limit_bytes=VMEM_LIMIT),
        name="in_proj_latent" if latent else "in_proj_context",
    )(*args)


def _attn_kernel(q_ref, kl_ref, kc_ref, vtl_ref, vtc_ref, o_ref):
    tile = q_ref.shape[1]
    rows = kl_ref.shape[1]
    n_lat = rows // KEY_BLOCK
    n_blocks = n_lat + kc_ref.shape[1] // KEY_BLOCK
    n_pass = N_Q_HEADS // HEADS_PER_PASS
    q_t = q_ref[0].T
    zeros = jnp.zeros((HEAD_DIM, tile), BF16)

    def key_block(ref, transposed, kv, j):
        src, j = (ref[0], j) if j < n_lat else (ref[1], j - n_lat)
        if transposed:
            return src[0, kv * HEAD_DIM:(kv + 1) * HEAD_DIM, j * KEY_BLOCK:(j + 1) * KEY_BLOCK]
        return src[0, j * KEY_BLOCK:(j + 1) * KEY_BLOCK, :]

    q_pads = {}

    def scores(pass_idx, j):
        kv = pass_idx * HEADS_PER_PASS // Q_PER_KV
        if pass_idx not in q_pads:
            q_pads[pass_idx] = jnp.concatenate(
                [jnp.concatenate([zeros] * kv + [q_t[hq * HEAD_DIM:(hq + 1) * HEAD_DIM, :]]
                                 + [zeros] * (N_KV_HEADS - 1 - kv), axis=0)
                 for hq in range(pass_idx * HEADS_PER_PASS, (pass_idx + 1) * HEADS_PER_PASS)], axis=1)
        return _dot(key_block((kl_ref, kc_ref), False, kv, j), q_pads[pass_idx])

    def absorb(pass_idx, j, s, state):
        kv = pass_idx * HEADS_PER_PASS // Q_PER_KV
        block_max = jnp.max(s, axis=0, keepdims=True)
        if state is None:
            m_new = block_max
        else:
            m_old, l_old, acc_old = state
            m_new = jnp.maximum(m_old, block_max)
            alpha = jnp.exp2(m_old - m_new)
        p = jnp.exp2(s - m_new)
        l_new = jnp.sum(p, axis=0, keepdims=True)
        acc = _dot(key_block((vtl_ref, vtc_ref), True, kv, j), p.astype(BF16))
        if state is not None:
            l_new = l_new + alpha * l_old
            acc = acc + alpha * acc_old
        return m_new, l_new, acc

    stages = [(p_, j) for p_ in range(n_pass) for j in range(n_blocks)]
    outs = []
    state = None
    pending = [scores(*stages[i]) for i in range(LOOKAHEAD)]
    for i, (pass_idx, j) in enumerate(stages):
        if i + LOOKAHEAD < len(stages):
            pending.append(scores(*stages[i + LOOKAHEAD]))
        state = absorb(pass_idx, j, pending.pop(0), state)
        if j == n_blocks - 1:
            _, l_fin, acc = state
            o_t = acc * (1.0 / l_fin)
            outs += [o_t[:, t * tile:(t + 1) * tile] for t in range(HEADS_PER_PASS)]
            state = None
    o_ref[0] = jnp.concatenate(outs, axis=0).T.astype(BF16)


def _attn_call(q, k_l, k_c, vt_l, vt_c, *, tile):
    bsz, rows, _ = q.shape
    lc = k_c.shape[1]
    per_batch = lambda shape: pl.BlockSpec((1,) + shape, lambda b, j: (b, 0, 0))
    return pl.pallas_call(
        _attn_kernel,
        grid=(bsz, rows // tile),
        in_specs=[pl.BlockSpec((1, tile, ATTN_WIDTH), lambda b, j: (b, j, 0)),
                  per_batch((rows, KV_WIDTH)), per_batch((lc, KV_WIDTH)),
                  per_batch((KV_WIDTH, rows)), per_batch((KV_WIDTH, lc))],
        out_specs=pl.BlockSpec((1, tile, ATTN_WIDTH), lambda b, j: (b, j, 0)),
        out_shape=jax.ShapeDtypeStruct((bsz, rows, ATTN_WIDTH), BF16),
        compiler_params=pltpu.CompilerParams(dimension_semantics=("arbitrary", "arbitrary"),
                                             vmem_limit_bytes=VMEM_LIMIT),
        name="gqa_attention",
    )(q, k_l, k_c, vt_l, vt_c)


def _conv_chunk(src_ref, dst_ref, cw_ref, cb_ref, shift_ref, i, n_chunks):
    i = jnp.asarray(i, jnp.int32)
    row0 = pl.multiple_of(i * CHUNK, CHUNK)
    main = src_ref[0, pl.ds(row0, CHUNK), :]
    prev0 = pl.multiple_of(jnp.maximum(row0 - CONV_HALO, 0), CONV_HALO)
    next0 = pl.multiple_of(jnp.minimum(row0 + CHUNK, (n_chunks - 1) * CHUNK + CHUNK - CONV_HALO), CONV_HALO)
    prev = src_ref[0, pl.ds(prev0, CONV_HALO), :] * (i > 0).astype(BF16)
    nxt = src_ref[0, pl.ds(next0, CONV_HALO), :] * (i < n_chunks - 1).astype(BF16)
    fill = jnp.zeros((CONV_WINDOW - CHUNK - 2 * CONV_HALO, CONV_CH), BF16)
    win = jnp.concatenate([prev, main, nxt, fill], axis=0)
    off_taps = [k for k in range(D_CONV) if k != CONV_PAD]
    for c0 in range(0, CONV_CH, GROUP_WIDTH):
        cols = slice(c0, c0 + GROUP_WIDTH)
        shifted = _dot(shift_ref[...], win[:, cols])
        acc = cb_ref[:, cols] + cw_ref[CONV_PAD:CONV_PAD + 1, cols] * main[:, cols].astype(F32)
        for j, k in enumerate(off_taps):
            acc = acc + cw_ref[k:k + 1, cols] * shifted[j * CHUNK:(j + 1) * CHUNK, :]
        dst_ref[pl.ds(row0, CHUNK), cols] = _silu(acc).astype(BF16)


def _pack3(v):
    p1, p2, p3 = _split3(v)
    packed = p1.astype(F32) + pltpu.roll(p2.astype(F32), PIECE_LANES, axis=1) \
        + pltpu.roll(p3.astype(F32), 2 * PIECE_LANES, axis=1)
    return packed.astype(BF16)


def _chunk_terms(dt_ref, terms, i, aneg2, tri2, lane_ok, lane_fwd, piece_fwd):
    acs_ref, pk_dt_ref, pk_end_ref, pk_start_ref = terms
    rows = pl.ds(pl.multiple_of(jnp.asarray(i, jnp.int32) * CHUNK, CHUNK), CHUNK)
    dtc = dt_ref[0, rows, :]
    r = _dot(tri2, _pack3(dtc * aneg2))
    r = jnp.where(piece_fwd, r[:CHUNK], r[CHUNK:])
    a_cs = (r + pltpu.roll(r, LANES - PIECE_LANES, axis=1) + pltpu.roll(r, LANES - 2 * PIECE_LANES, axis=1)) * lane_ok
    total = jnp.where(lane_fwd, a_cs[CHUNK - 1:CHUNK, :], a_cs[0:1, :])
    acs_ref[rows, :] = a_cs
    pk_dt_ref[rows, :] = _pack3(dtc)
    pk_end_ref[rows, :] = _pack3(dtc * jnp.exp2(total - a_cs) * lane_ok)
    pk_start_ref[rows, :] = _pack3(jnp.exp2(a_cs) * lane_ok)


def _chunk_cb(act_ref, cb_ref, i):
    rows = pl.ds(pl.multiple_of(jnp.asarray(i, jnp.int32) * CHUNK, CHUNK), CHUNK)
    for g in range(SSD_GROUPS):
        bg = act_ref[rows, SSD_WIDTH + g * D_STATE:SSD_WIDTH + (g + 1) * D_STATE]
        cg = act_ref[rows, SSD_WIDTH + GN + g * D_STATE:SSD_WIDTH + GN + (g + 1) * D_STATE]
        cb_ref[rows, g * CHUNK:(g + 1) * CHUNK] = lax.dot_general(
            cg, bg, (((1,), (1,)), ((), ())), preferred_element_type=F32)


def _ssd_chunk(act_ref, terms, cb_ref, row0, direction, want_y, expand, state_ref):
    acs_ref, pk_dt_ref, pk_end_ref, pk_start_ref = terms
    rows = pl.ds(row0, CHUNK)
    last = CHUNK - 1 if direction == 0 else 0
    a_cs = acs_ref[rows, :]
    dt_end_b = _dot(pk_end_ref[rows, :], expand)
    start_b = _dot(pk_start_ref[rows, :], expand)
    chunk_decay_b = start_b[last:last + 1, :]
    xs = act_ref[rows, :SSD_WIDTH].astype(F32)
    bm = act_ref[rows, SSD_WIDTH:SSD_WIDTH + GN]
    x_end = (xs * dt_end_b).astype(BF16)
    state = state_ref[...]
    made = [lax.dot_general(bm[:, g * D_STATE:(g + 1) * D_STATE],
                            x_end[:, g * GROUP_WIDTH:(g + 1) * GROUP_WIDTH],
                            (((0,), (0,)), ((), ())), preferred_element_type=F32)
            for g in range(SSD_GROUPS)]
    state_ref[...] = state * chunk_decay_b + jnp.concatenate(made, axis=1)
    if not want_y:
        return None
    cm = act_ref[rows, SSD_WIDTH + GN:]
    x_bf = (xs * _dot(pk_dt_ref[rows, :], expand)).astype(BF16)
    state_bf = state.astype(BF16)
    a_cs_t = a_cs.T
    li = lax.broadcasted_iota(jnp.int32, (CHUNK, CHUNK), 0)
    si = lax.broadcasted_iota(jnp.int32, (CHUNK, CHUNK), 1)
    visible = (si <= li) if direction == 0 else (si >= li)
    slab_lane = lax.broadcasted_iota(jnp.int32, (CHUNK, KV_WIDTH), 1)
    heads_per_group = SSD_HEADS // SSD_GROUPS
    heads_per_slab = KV_WIDTH // SSD_HEAD_DIM
    slabs = []
    for g in range(SSD_GROUPS):
        cg = cm[:, g * D_STATE:(g + 1) * D_STATE]
        gcols = slice(g * GROUP_WIDTH, (g + 1) * GROUP_WIDTH)
        y_off = _dot(cg, state_bf[:, gcols]) * start_b[:, gcols]
        cb = cb_ref[rows, g * CHUNK:(g + 1) * CHUNK]
        for sl in range(GROUP_WIDTH // KV_WIDTH):
            slab0 = g * GROUP_WIDTH + sl * KV_WIDTH
            x_slab = x_bf[:, slab0:slab0 + KV_WIDTH]
            acc = y_off[:, sl * KV_WIDTH:(sl + 1) * KV_WIDTH]
            for pair in range(heads_per_slab // 2):
                mats, xheads = [], []
                for t in (2 * pair, 2 * pair + 1):
                    hd = g * heads_per_group + sl * heads_per_slab + t
                    lane = direction * SSD_HEADS + hd
                    seg = a_cs[:, lane:lane + 1] - a_cs_t[lane:lane + 1, :]
                    lmat = jnp.exp2(jnp.where(visible, seg, -jnp.inf))
                    mats.append((cb * lmat).astype(BF16))
                    in_head = (slab_lane >= t * SSD_HEAD_DIM) & (slab_lane < (t + 1) * SSD_HEAD_DIM)
                    xheads.append(jnp.where(in_head, x_slab, jnp.zeros_like(x_slab)))
                acc = acc + _dot(jnp.concatenate(mats, axis=1), jnp.concatenate(xheads, axis=0))
            slabs.append(acc)
    return jnp.concatenate(slabs, axis=1)


def _ssd_kernel(xl_ref, dtl_ref, xc_ref, dtc_ref, cw_ref, cb_ref, alog_ref, dskip_ref, shift_ref, tri2_ref,
                e_fwd_ref, e_bwd_ref, y_ref, act_l, act_c, y_part, state_f, state_b,
                acs_l, pkdt_l, pkend_l, pkstart_l, acs_c, pkdt_c, pkend_c, pkstart_c, cbt_l, *, n_lat, n_ctx):
    aneg2 = -jnp.exp(alog_ref[...]) * LOG2E
    dt_lane = lax.broadcasted_iota(jnp.int32, (1, DT_PAD), 1)
    lane_ok = (dt_lane < 2 * SSD_HEADS).astype(F32)
    lane_fwd = dt_lane < SSD_HEADS
    piece_fwd = (dt_lane % PIECE_LANES) < SSD_HEADS
    terms_l = (acs_l, pkdt_l, pkend_l, pkstart_l)
    terms_c = (acs_c, pkdt_c, pkend_c, pkstart_c)

    def prepare(src_ref, dt_ref, act_ref, terms, cbt_ref, i, n_chunks):
        _conv_chunk(src_ref, act_ref, cw_ref, cb_ref, shift_ref, i, n_chunks)
        _chunk_terms(dt_ref, terms, i, aneg2, tri2_ref[...], lane_ok, lane_fwd, piece_fwd)
        if cbt_ref is not None:
            _chunk_cb(act_ref, cbt_ref, i)

    lax.fori_loop(0, n_lat, lambda i, c: (prepare(xl_ref, dtl_ref, act_l, terms_l, cbt_l, i, n_lat), c)[1], 0,
                  unroll=PREP_UNROLL)
    for i in range(n_ctx):
        prepare(xc_ref, dtc_ref, act_c, terms_c, None, i, n_ctx)
    expands = (e_fwd_ref[...], e_bwd_ref[...])
    states = (state_f, state_b)
    for direction in range(2):
        states[direction][...] = jnp.zeros_like(states[direction])
        for i in range(n_ctx):
            chunk = i if direction == 0 else n_ctx - 1 - i
            _ssd_chunk(act_c, terms_c, None, chunk * CHUNK, direction, False, expands[direction], states[direction])

    def step(i, second_half):
        for direction in range(2):
            chunk = i if direction == 0 else n_lat - 1 - i
            row0 = pl.multiple_of(chunk * CHUNK, CHUNK)
            y = _ssd_chunk(act_l, terms_l, cbt_l, row0, direction, True, expands[direction], states[direction])
            rows = pl.ds(row0, CHUNK)
            if second_half:
                xs = act_l[rows, :SSD_WIDTH].astype(F32)
                y_ref[0, rows, :] = (y_part[rows, :] + y + dskip_ref[...] * xs).astype(BF16)
            else:
                y_part[rows, :] = y

    half = n_lat // 2
    lax.fori_loop(0, half, lambda i, c: (step(i, False), c)[1], 0, unroll=SCAN_UNROLL)
    lax.fori_loop(half, n_lat, lambda i, c: (step(i, True), c)[1], 0, unroll=SCAN_UNROLL)


def _ssd_call(xbc_l, dt_l, xbc_c, dt_c, consts):
    bsz, rows, _ = xbc_l.shape
    lc = xbc_c.shape[1]
    per_batch = lambda shape: pl.BlockSpec((1,) + shape, lambda b: (b, 0, 0))
    assert rows % (2 * CHUNK) == 0 and lc % CHUNK == 0
    body = functools.partial(_ssd_kernel, n_lat=rows // CHUNK, n_ctx=lc // CHUNK)
    return pl.pallas_call(
        body,
        grid=(bsz,),
        in_specs=[per_batch((rows, CONV_CH)), per_batch((rows, DT_PAD)),
                  per_batch((lc, CONV_CH)), per_batch((lc, DT_PAD)),
                  _const_spec((8, CONV_CH)), _const_spec((1, CONV_CH)), _const_spec((1, DT_PAD)),
                  _const_spec((1, SSD_WIDTH)), _const_spec(((D_CONV - 1) * CHUNK, CONV_WINDOW)),
                  _const_spec((2 * CHUNK, CHUNK)),
                  _const_spec((DT_PAD, SSD_WIDTH)), _const_spec((DT_PAD, SSD_WIDTH))],
        out_specs=per_batch((rows, SSD_WIDTH)),
        out_shape=jax.ShapeDtypeStruct((bsz, rows, SSD_WIDTH), BF16),
        scratch_shapes=[pltpu.VMEM((rows, CONV_CH), BF16), pltpu.VMEM((lc, CONV_CH), BF16),
                        pltpu.VMEM((rows, SSD_WIDTH), F32),
                        pltpu.VMEM((D_STATE, SSD_WIDTH), F32), pltpu.VMEM((D_STATE, SSD_WIDTH), F32)]
                       + [pltpu.VMEM((n, DT_PAD), dt) for n in (rows, lc) for dt in (F32, BF16, BF16, BF16)]
                       + [pltpu.VMEM((rows, SSD_GROUPS * CHUNK), F32)],
        compiler_params=pltpu.CompilerParams(dimension_semantics=("arbitrary",),
                                             vmem_limit_bytes=VMEM_LIMIT),
        name="ssd_bidir",
    )(xbc_l, dt_l, xbc_c, dt_c, consts["conv_w"], consts["conv_b"], consts["a_log"], consts["d_skip"],
      consts["shift"], consts["tri2"], consts["e_fwd"], consts["e_bwd"])


def _out_kernel(o_ref, ga_ref, y_ref, z_ref, x_ref, mod_ref, wa_ref, ws_ref, sg_ref, fg_ref, out_ref):
    n_sub = o_ref.shape[1] // OUT_SUB
    gate = mod_ref[0][:, 2 * D_MODEL:]

    def gated(c):
        r = slice(c * OUT_SUB, (c + 1) * OUT_SUB)
        y_a = o_ref[0, r, :] * ga_ref[0, r, :]
        t = y_ref[0, r, :].astype(F32) * z_ref[0, r, :].astype(F32)
        y_s = (t * lax.rsqrt(jnp.mean(t * t, axis=-1, keepdims=True) + EPS) * sg_ref[...]).astype(BF16)
        return y_a, y_s

    def project(c, y_a, y_s):
        r = slice(c * OUT_SUB, (c + 1) * OUT_SUB)
        new = x_ref[0, r, :] + gate * (_dot(y_a, wa_ref[...]) + _dot(y_s, ws_ref[...]))
        out_ref[0, r, :] = new * lax.rsqrt(jnp.mean(new * new, axis=-1, keepdims=True) + EPS) * fg_ref[...]

    ready = gated(0)
    for c in range(n_sub):
        upcoming = gated(c + 1) if c + 1 < n_sub else None
        project(c, *ready)
        ready = upcoming


def _out_call(o_attn, ga, y_ssd, z, x, mod, w_a, w_s, ssd_gain, final_gain, *, tile):
    bsz, rows, _ = x.shape
    tok = pl.BlockSpec((1, tile, D_MODEL), lambda b, j: (b, j, 0))
    return pl.pallas_call(
        _out_kernel,
        grid=(bsz, rows // tile),
        in_specs=[tok, tok, tok, tok, tok,
                  pl.BlockSpec((1, 1, 3 * D_MODEL), lambda b, j: (b, 0, 0)),
                  _const_spec((ATTN_WIDTH, D_MODEL)), _const_spec((SSD_WIDTH, D_MODEL)),
                  _const_spec((1, SSD_WIDTH)), _const_spec((1, D_MODEL))],
        out_specs=tok,
        out_shape=jax.ShapeDtypeStruct((bsz, rows, D_MODEL), F32),
        compiler_params=pltpu.CompilerParams(dimension_semantics=("arbitrary", "arbitrary"),
                                             vmem_limit_bytes=VMEM_LIMIT),
        name="merge_out_proj",
    )(o_attn, ga, y_ssd, z, x, mod, w_a, w_s, ssd_gain, final_gain)


def _rope_tables(rows):
    n_freq = HEAD_DIM // 4
    t = jnp.arange(rows, dtype=jnp.int32)
    pos = jnp.stack([(t // GRID_W).astype(F32), (t % GRID_W).astype(F32)], axis=1)
    inv_freq = ROPE_THETA ** (-jnp.arange(n_freq, dtype=F32) / n_freq)
    ang = pos[:, :, None] * inv_freq
    cos = jnp.cos(ang)[:, :, None, :]
    sin = jnp.sin(ang)[:, :, None, :]
    cos_h = jnp.broadcast_to(cos, (rows, 2, 2, n_freq)).reshape(rows, HEAD_DIM)
    sin_h = jnp.concatenate([-sin, sin], axis=2).reshape(rows, HEAD_DIM)
    reps = LANES // HEAD_DIM
    return jnp.tile(cos_h, (1, reps)), jnp.tile(sin_h, (1, reps))


def _pad_lanes(v, width):
    return jnp.pad(v, [(0, 0)] * (v.ndim - 1) + [(0, width - v.shape[-1])])


def kernel(x, c, ctx, c_ctx, ada_w, ada_b, norm_g, w_in, conv_w, conv_b, dt_bias, a_log, d_skip,
           q_norm_g, k_norm_g, ssd_norm_g, w_out, final_g):
    assert ada_w.shape[0] == 1, "single-layer problem: context outputs are never needed"
    bsz, rows, _ = x.shape
    ada_w, ada_b, norm_g, w_in, conv_w, conv_b = ada_w[0], ada_b[0], norm_g[0], w_in[0], conv_w[0], conv_b[0]
    dt_bias, a_log, d_skip = dt_bias[0], a_log[0], d_skip[0]
    q_norm_g, k_norm_g, ssd_norm_g, w_out = q_norm_g[0], k_norm_g[0], ssd_norm_g[0], w_out[0]

    n_rows = -(-(bsz + 1) // 8) * 8
    cc = jnp.concatenate([c, c_ctx[None, :], jnp.zeros((n_rows - bsz - 1, D_MODEL), F32)], axis=0)
    mod = _mod_call(cc, ada_w, ada_b)
    mod_l = mod[:bsz].reshape(bsz, 1, 3 * D_MODEL)
    mod_c = mod[bsz:bsz + 1].reshape(1, 1, 3 * D_MODEL)

    w_bf = w_in.astype(BF16)
    weights = {
        "k": w_bf[:, :OFF_V], "v": w_bf[:, OFF_V:OFF_XBC], "xbc": w_bf[:, OFF_XBC:OFF_DT],
        "dt": _pad_lanes(w_bf[:, OFF_DT:CTX_COLS], DT_PAD),
        "q": w_bf[:, OFF_Q:OFF_GA], "ga": w_bf[:, OFF_GA:OFF_Z], "z": w_bf[:, OFF_Z:],
    }
    cos, sin = _rope_tables(rows)
    head_of_lane = np.arange(KV_WIDTH) // HEAD_DIM
    consts = {
        "dt_bias": _pad_lanes(dt_bias.reshape(1, 2 * SSD_HEADS), DT_PAD),
        "k_gain": jnp.tile(k_norm_g, KV_WIDTH // HEAD_DIM).reshape(1, KV_WIDTH),
        "q_gain": jnp.tile(q_norm_g, KV_WIDTH // HEAD_DIM).reshape(1, KV_WIDTH),
        "bd": jnp.asarray(head_of_lane[:, None] == head_of_lane[None, :], BF16),
        "cos": cos, "sin": sin,
    }
    ng = norm_g.reshape(1, D_MODEL)
    vt_c, k_c, xbc_c, dt_c = _in_proj_call(ctx, mod_c, ng, weights, consts, latent=False,
                                           tile=min(IN_TILE, ctx.shape[1]))
    vt_l, k_l, xbc_l, dt_l, q, ga, z = _in_proj_call(x, mod_l, ng, weights, consts, latent=True, tile=IN_TILE)

    o_attn = _attn_call(q, k_l, k_c, vt_l, vt_c, tile=ATTN_TILE)

    tok = np.arange(CHUNK)
    lane_head = np.arange(SSD_WIDTH) // SSD_HEAD_DIM
    dt_lane = np.arange(DT_PAD)
    ssd_consts = {
        "conv_w": jnp.pad(conv_w, ((0, 8 - D_CONV), (0, 0))),
        "conv_b": conv_b.reshape(1, CONV_CH),
        "a_log": _pad_lanes(a_log.reshape(1, 2 * SSD_HEADS), DT_PAD),
        "d_skip": jnp.repeat(d_skip, SSD_HEAD_DIM).reshape(1, SSD_WIDTH),
        "shift": jnp.asarray(np.concatenate(
            [np.arange(CONV_WINDOW)[None, :] == (CONV_HALO + tok[:, None] + k - CONV_PAD)
             for k in range(D_CONV) if k != CONV_PAD], axis=0), BF16),
        "tri2": jnp.asarray(np.concatenate([tok[None, :] <= tok[:, None], tok[None, :] >= tok[:, None]], axis=0), BF16),
        "e_fwd": jnp.asarray((dt_lane[:, None] % PIECE_LANES == lane_head[None, :])
                             & (dt_lane[:, None] < 3 * PIECE_LANES), BF16),
        "e_bwd": jnp.asarray((dt_lane[:, None] % PIECE_LANES == lane_head[None, :] + SSD_HEADS)
                             & (dt_lane[:, None] < 3 * PIECE_LANES), BF16),
    }
    y_ssd = _ssd_call(xbc_l, dt_l, xbc_c, dt_c, ssd_consts)

    w_out_bf = w_out.astype(BF16)
    return _out_call(o_attn, ga, y_ssd, z, x, mod_l, w_out_bf[:ATTN_WIDTH], w_out_bf[ATTN_WIDTH:],
                     ssd_norm_g.reshape(1, SSD_WIDTH), final_g.reshape(1, D_MODEL), tile=OUT_TILE)
```

```python
import functools

import jax
import jax.numpy as jnp
import numpy as np
from jax import lax
from jax.experimental import pallas as pl
from jax.experimental.pallas import tpu as pltpu

F32 = jnp.float32
BF16 = jnp.bfloat16

D_MODEL = 1024
GRID_W = 64
HEAD_DIM = 64
N_Q_HEADS = 16
N_KV_HEADS = 4
Q_PER_KV = N_Q_HEADS // N_KV_HEADS
ATTN_WIDTH = N_Q_HEADS * HEAD_DIM
KV_WIDTH = N_KV_HEADS * HEAD_DIM
SSD_HEADS = 16
SSD_HEAD_DIM = 64
SSD_WIDTH = SSD_HEADS * SSD_HEAD_DIM
SSD_GROUPS = 2
GROUP_WIDTH = SSD_WIDTH // SSD_GROUPS
D_STATE = 128
GN = SSD_GROUPS * D_STATE
D_CONV = 5
CONV_PAD = D_CONV // 2
CONV_CH = SSD_WIDTH + 2 * GN
CHUNK = 128
ROPE_THETA = 10000.0
ATTN_SCALE = HEAD_DIM ** -0.5
EPS = 1e-6
OFF_V = KV_WIDTH
OFF_XBC = 2 * KV_WIDTH
OFF_DT = OFF_XBC + CONV_CH
CTX_COLS = OFF_DT + 2 * SSD_HEADS
OFF_Q = CTX_COLS
OFF_GA = OFF_Q + ATTN_WIDTH
OFF_Z = OFF_GA + ATTN_WIDTH

LANES = 128
DT_PAD = LANES
CONV_HALO = 16
CONV_WINDOW = 256
PIECE_LANES = 2 * SSD_HEADS
LOG2E = 1.4426950408889634
VMEM_LIMIT = 56 * 1024 * 1024

IN_TILE = 1024
OUT_TILE = 1024
IN_SUB = 1024
OUT_SUB = 256
ATTN_TILE = 256
KEY_BLOCK = 256
LOOKAHEAD = 3
HEADS_PER_PASS = 2
PREP_UNROLL = 4
SCAN_UNROLL = 2


def _silu(v):
    return v * (1.0 / (1.0 + jnp.exp2(v * (-LOG2E))))


def _softplus(v):
    return jnp.maximum(v, 0.0) + jnp.log1p(jnp.exp(-jnp.abs(v)))


def _split3(v):
    p1 = v.astype(BF16)
    r1 = v - p1.astype(F32)
    p2 = r1.astype(BF16)
    p3 = (r1 - p2.astype(F32)).astype(BF16)
    return p1, p2, p3


def _dot(a, b):
    return jnp.dot(a, b, preferred_element_type=F32)


def _const_spec(shape):
    nd = len(shape)
    return pl.BlockSpec(shape, lambda *_: (0,) * nd, pipeline_mode=pl.Buffered(1))


def _mod_kernel(c_ref, w_ref, b_ref, o_ref):
    s = _silu(c_ref[...])
    o_ref[...] = jnp.dot(s, w_ref[...], preferred_element_type=F32,
                         precision=lax.Precision.HIGHEST) + b_ref[...]


def _mod_call(cc, ada_w, ada_b):
    rows = cc.shape[0]
    n = ada_w.shape[1]
    bn = D_MODEL
    return pl.pallas_call(
        _mod_kernel,
        grid=(n // bn,),
        in_specs=[pl.BlockSpec((rows, D_MODEL), lambda j: (0, 0)),
                  pl.BlockSpec((D_MODEL, bn), lambda j: (0, j)),
                  pl.BlockSpec((1, bn), lambda j: (0, j))],
        out_specs=pl.BlockSpec((rows, bn), lambda j: (0, j)),
        out_shape=jax.ShapeDtypeStruct((rows, n), F32),
        compiler_params=pltpu.CompilerParams(dimension_semantics=("arbitrary",),
                                             vmem_limit_bytes=VMEM_LIMIT),
        name="adaln_mod",
    )(cc, ada_w, ada_b.reshape(1, n))


def _modulated_norm(x, mod, ng):
    ms = jnp.mean(x * x, axis=-1, keepdims=True)
    y = (x * lax.rsqrt(ms + EPS)) * ng
    return y * (1.0 + mod[:, D_MODEL:2 * D_MODEL]) + mod[:, :D_MODEL]


def _head_norm(v, gain, bd):
    ssq = _dot((v * v).astype(BF16), bd)
    return v * lax.rsqrt(ssq * (1.0 / HEAD_DIM) + EPS) * gain


def _rope(v, cos, sin_signed):
    lane = lax.broadcasted_iota(jnp.int32, v.shape, 1)
    up = pltpu.roll(v, LANES - 16, axis=1)
    down = pltpu.roll(v, 16, axis=1)
    partner = jnp.where((lane & 16) == 0, up, down)
    return v * cos + partner * sin_signed


def _in_kernel(x_ref, mod_ref, ng_ref, wk_ref, wv_ref, wx_ref, wdt_ref, dtb_ref, kg_ref, bd_ref, *rest,
               latent, tile):
    if latent:
        wq_ref, wg_ref, wz_ref, qg_ref, cos_ref, sin_ref, vt_out, k_out, xbc_out, dt_out, q_out, ga_out, z_out = rest
        tile_row0 = pl.multiple_of(pl.program_id(1) * tile, tile)
    else:
        vt_out, k_out, xbc_out, dt_out = rest
    sub = min(IN_SUB, tile)

    def normed(c):
        return _modulated_norm(x_ref[0, c * sub:(c + 1) * sub, :], mod_ref[0], ng_ref[...]).astype(BF16)

    def rotated(v, cos, sin):
        if not latent:
            return v
        return jnp.concatenate([_rope(v[:, t * LANES:(t + 1) * LANES], cos, sin)
                                for t in range(KV_WIDTH // LANES)], axis=1)

    def project(c, h):
        r = slice(c * sub, (c + 1) * sub)
        cos = sin = None
        if latent:
            cos = cos_ref[pl.ds(tile_row0 + c * sub, sub), :]
            sin = sin_ref[pl.ds(tile_row0 + c * sub, sub), :]
            for s in range(ATTN_WIDTH // KV_WIDTH):
                cols = slice(s * KV_WIDTH, (s + 1) * KV_WIDTH)
                qn = _head_norm(_dot(h, wq_ref[:, cols]), qg_ref[...], bd_ref[...])
                q_out[0, r, cols] = (rotated(qn, cos, sin) * (ATTN_SCALE * LOG2E)).astype(BF16)
        kn = _head_norm(_dot(h, wk_ref[...]), kg_ref[...], bd_ref[...])
        k_out[0, r, :] = rotated(kn, cos, sin).astype(BF16)
        xbc_out[0, r, :] = _dot(h, wx_ref[...]).astype(BF16)
        if latent:
            ga_out[0, r, :] = _silu(_dot(h, wg_ref[...])).astype(BF16)
            z_out[0, r, :] = _silu(_dot(h, wz_ref[...])).astype(BF16)
        vt_out[0, :, r] = _dot(h, wv_ref[...]).T.astype(BF16)
        dt_raw = _dot(h, wdt_ref[...]) + dtb_ref[...]
        lane = lax.broadcasted_iota(jnp.int32, dt_raw.shape, 1)
        dt_out[0, r, :] = jnp.where(lane < 2 * SSD_HEADS, _softplus(dt_raw), 0.0)

    ready = normed(0)
    for c in range(tile // sub):
        upcoming = normed(c + 1) if (c + 1) * sub < tile else None
        project(c, ready)
        ready = upcoming


def _in_proj_call(x, mod, ng, w, consts, *, latent, tile):
    bsz, rows, _ = x.shape
    nt = rows // tile
    tok = lambda width: pl.BlockSpec((1, tile, width), lambda b, j: (b, j, 0))
    mod_map = (lambda b, j: (b, 0, 0)) if latent else (lambda b, j: (0, 0, 0))
    in_specs = [tok(D_MODEL),
                pl.BlockSpec((1, 1, 3 * D_MODEL), mod_map),
                _const_spec((1, D_MODEL)),
                _const_spec((D_MODEL, KV_WIDTH)), _const_spec((D_MODEL, KV_WIDTH)),
                _const_spec((D_MODEL, CONV_CH)), _const_spec((D_MODEL, DT_PAD)),
                _const_spec((1, DT_PAD)), _const_spec((1, KV_WIDTH)), _const_spec((KV_WIDTH, KV_WIDTH))]
    args = [x, mod, ng, w["k"], w["v"], w["xbc"], w["dt"], consts["dt_bias"], consts["k_gain"], consts["bd"]]
    out_specs = [pl.BlockSpec((1, KV_WIDTH, tile), lambda b, j: (b, 0, j)),
                 tok(KV_WIDTH), tok(CONV_CH), tok(DT_PAD)]
    out_shape = [jax.ShapeDtypeStruct((bsz, KV_WIDTH, rows), BF16),
                 jax.ShapeDtypeStruct((bsz, rows, KV_WIDTH), BF16),
                 jax.ShapeDtypeStruct((bsz, rows, CONV_CH), BF16),
                 jax.ShapeDtypeStruct((bsz, rows, DT_PAD), F32)]
    if latent:
        in_specs += [_const_spec((D_MODEL, ATTN_WIDTH)), _const_spec((D_MODEL, ATTN_WIDTH)),
                     _const_spec((D_MODEL, SSD_WIDTH)), _const_spec((1, KV_WIDTH)),
                     _const_spec((rows, LANES)), _const_spec((rows, LANES))]
        args += [w["q"], w["ga"], w["z"], consts["q_gain"], consts["cos"], consts["sin"]]
        out_specs += [tok(ATTN_WIDTH), tok(ATTN_WIDTH), tok(SSD_WIDTH)]
        out_shape += [jax.ShapeDtypeStruct((bsz, rows, ATTN_WIDTH), BF16),
                      jax.ShapeDtypeStruct((bsz, rows, ATTN_WIDTH), BF16),
                      jax.ShapeDtypeStruct((bsz, rows, SSD_WIDTH), BF16)]
    body = functools.partial(_in_kernel, latent=latent, tile=tile)
    return pl.pallas_call(
        body,
        grid=(bsz, nt),
        in_specs=in_specs,
        out_specs=out_specs,
        out_shape=out_shape,
        compiler_params=pltpu.CompilerParams(dimension_semantics=("arbitrary", "arbitrary"),
                                             vmem_limit_bytes=VMEM_LIMIT),
        name="in_proj_latent" if latent else "in_proj_context",
    )(*args)


def _attn_kernel(q_ref, ga_ref, kl_ref, kc_ref, vtl_ref, vtc_ref, o_ref):
    tile = q_ref.shape[1]
    rows = kl_ref.shape[1]
    n_lat = rows // KEY_BLOCK
    n_blocks = n_lat + kc_ref.shape[1] // KEY_BLOCK
    n_pass = N_Q_HEADS // HEADS_PER_PASS
    q_t = q_ref[0].T
    zeros = jnp.zeros((HEAD_DIM, tile), BF16)

    def key_block(ref, transposed, kv, j):
        src, j = (ref[0], j) if j < n_lat else (ref[1], j - n_lat)
        if transposed:
            return src[0, kv * HEAD_DIM:(kv + 1) * HEAD_DIM, j * KEY_BLOCK:(j + 1) * KEY_BLOCK]
        return src[0, j * KEY_BLOCK:(j + 1) * KEY_BLOCK, :]

    q_pads = {}

    def scores(pass_idx, j):
        kv = pass_idx * HEADS_PER_PASS // Q_PER_KV
        if pass_idx not in q_pads:
            q_pads[pass_idx] = jnp.concatenate(
                [jnp.concatenate([zeros] * kv + [q_t[hq * HEAD_DIM:(hq + 1) * HEAD_DIM, :]]
                                 + [zeros] * (N_KV_HEADS - 1 - kv), axis=0)
                 for hq in range(pass_idx * HEADS_PER_PASS, (pass_idx + 1) * HEADS_PER_PASS)], axis=1)
        return _dot(key_block((kl_ref, kc_ref), False, kv, j), q_pads[pass_idx])

    def absorb(pass_idx, j, s, state):
        kv = pass_idx * HEADS_PER_PASS // Q_PER_KV
        block_max = jnp.max(s, axis=0, keepdims=True)
        if state is None:
            m_new = block_max
        else:
            m_old, l_old, acc_old = state
            m_new = jnp.maximum(m_old, block_max)
            alpha = jnp.exp2(m_old - m_new)
        p = jnp.exp2(s - m_new)
        l_new = jnp.sum(p, axis=0, keepdims=True)
        acc = _dot(key_block((vtl_ref, vtc_ref), True, kv, j), p.astype(BF16))
        if state is not None:
            l_new = l_new + alpha * l_old
            acc = acc + alpha * acc_old
        return m_new, l_new, acc

    stages = [(p_, j) for p_ in range(n_pass) for j in range(n_blocks)]
    outs = []
    state = None
    pending = [scores(*stages[i]) for i in range(LOOKAHEAD)]
    for i, (pass_idx, j) in enumerate(stages):
        if i + LOOKAHEAD < len(stages):
            pending.append(scores(*stages[i + LOOKAHEAD]))
        state = absorb(pass_idx, j, pending.pop(0), state)
        if j == n_blocks - 1:
            _, l_fin, acc = state
            o_t = acc * (1.0 / l_fin)
            outs += [o_t[:, t * tile:(t + 1) * tile] for t in range(HEADS_PER_PASS)]
            state = None
    o_ref[0] = jnp.concatenate(outs, axis=0).T.astype(BF16) * ga_ref[0]


def _attn_call(q, ga, k_l, k_c, vt_l, vt_c, *, tile):
    bsz, rows, _ = q.shape
    lc = k_c.shape[1]
    per_batch = lambda shape: pl.BlockSpec((1,) + shape, lambda b, j: (b, 0, 0))
    tok = pl.BlockSpec((1, tile, ATTN_WIDTH), lambda b, j: (b, j, 0))
    return pl.pallas_call(
        _attn_kernel,
        grid=(bsz, rows // tile),
        in_specs=[tok, tok,
                  per_batch((rows, KV_WIDTH)), per_batch((lc, KV_WIDTH)),
                  per_batch((KV_WIDTH, rows)), per_batch((KV_WIDTH, lc))],
        out_specs=pl.BlockSpec((1, tile, ATTN_WIDTH), lambda b, j: (b, j, 0)),
        out_shape=jax.ShapeDtypeStruct((bsz, rows, ATTN_WIDTH), BF16),
        compiler_params=pltpu.CompilerParams(dimension_semantics=("arbitrary", "arbitrary"),
                                             vmem_limit_bytes=VMEM_LIMIT),
        name="gqa_attention",
    )(q, ga, k_l, k_c, vt_l, vt_c)


def _conv_chunk(src_ref, dst_ref, cw_ref, cb_ref, shift_ref, i, n_chunks):
    i = jnp.asarray(i, jnp.int32)
    row0 = pl.multiple_of(i * CHUNK, CHUNK)
    main = src_ref[0, pl.ds(row0, CHUNK), :]
    prev0 = pl.multiple_of(jnp.maximum(row0 - CONV_HALO, 0), CONV_HALO)
    next0 = pl.multiple_of(jnp.minimum(row0 + CHUNK, (n_chunks - 1) * CHUNK + CHUNK - CONV_HALO), CONV_HALO)
    prev = src_ref[0, pl.ds(prev0, CONV_HALO), :] * (i > 0).astype(BF16)
    nxt = src_ref[0, pl.ds(next0, CONV_HALO), :] * (i < n_chunks - 1).astype(BF16)
    fill = jnp.zeros((CONV_WINDOW - CHUNK - 2 * CONV_HALO, CONV_CH), BF16)
    win = jnp.concatenate([prev, main, nxt, fill], axis=0)
    off_taps = [k for k in range(D_CONV) if k != CONV_PAD]
    for c0 in range(0, CONV_CH, GROUP_WIDTH):
        cols = slice(c0, c0 + GROUP_WIDTH)
        shifted = _dot(shift_ref[...], win[:, cols])
        acc = cb_ref[:, cols] + cw_ref[CONV_PAD:CONV_PAD + 1, cols] * main[:, cols].astype(F32)
        for j, k in enumerate(off_taps):
            acc = acc + cw_ref[k:k + 1, cols] * shifted[j * CHUNK:(j + 1) * CHUNK, :]
        dst_ref[pl.ds(row0, CHUNK), cols] = _silu(acc).astype(BF16)


def _pack3(v):
    p1, p2, p3 = _split3(v)
    packed = p1.astype(F32) + pltpu.roll(p2.astype(F32), PIECE_LANES, axis=1) \
        + pltpu.roll(p3.astype(F32), 2 * PIECE_LANES, axis=1)
    return packed.astype(BF16)


def _chunk_terms(dt_ref, terms, i, aneg2, tri2, lane_ok, lane_fwd, piece_fwd):
    acs_ref, pk_dt_ref, pk_end_ref, pk_start_ref = terms
    rows = pl.ds(pl.multiple_of(jnp.asarray(i, jnp.int32) * CHUNK, CHUNK), CHUNK)
    dtc = dt_ref[0, rows, :]
    r = _dot(tri2, _pack3(dtc * aneg2))
    r = jnp.where(piece_fwd, r[:CHUNK], r[CHUNK:])
    a_cs = (r + pltpu.roll(r, LANES - PIECE_LANES, axis=1) + pltpu.roll(r, LANES - 2 * PIECE_LANES, axis=1)) * lane_ok
    total = jnp.where(lane_fwd, a_cs[CHUNK - 1:CHUNK, :], a_cs[0:1, :])
    acs_ref[rows, :] = a_cs
    pk_dt_ref[rows, :] = _pack3(dtc)
    pk_end_ref[rows, :] = _pack3(dtc * jnp.exp2(total - a_cs) * lane_ok)
    pk_start_ref[rows, :] = _pack3(jnp.exp2(a_cs) * lane_ok)


def _chunk_cb(act_ref, cb_ref, i):
    rows = pl.ds(pl.multiple_of(jnp.asarray(i, jnp.int32) * CHUNK, CHUNK), CHUNK)
    for g in range(SSD_GROUPS):
        bg = act_ref[rows, SSD_WIDTH + g * D_STATE:SSD_WIDTH + (g + 1) * D_STATE]
        cg = act_ref[rows, SSD_WIDTH + GN + g * D_STATE:SSD_WIDTH + GN + (g + 1) * D_STATE]
        cb_ref[rows, g * CHUNK:(g + 1) * CHUNK] = lax.dot_general(
            cg, bg, (((1,), (1,)), ((), ())), preferred_element_type=F32)


def _ssd_chunk(act_ref, terms, cb_ref, row0, direction, want_y, expand, state_ref):
    acs_ref, pk_dt_ref, pk_end_ref, pk_start_ref = terms
    rows = pl.ds(row0, CHUNK)
    last = CHUNK - 1 if direction == 0 else 0
    a_cs = acs_ref[rows, :]
    dt_end_b = _dot(pk_end_ref[rows, :], expand)
    start_b = _dot(pk_start_ref[rows, :], expand)
    chunk_decay_b = start_b[last:last + 1, :]
    xs = act_ref[rows, :SSD_WIDTH].astype(F32)
    bm = act_ref[rows, SSD_WIDTH:SSD_WIDTH + GN]
    x_end = (xs * dt_end_b).astype(BF16)
    state = state_ref[...]
    made = [lax.dot_general(bm[:, g * D_STATE:(g + 1) * D_STATE],
                            x_end[:, g * GROUP_WIDTH:(g + 1) * GROUP_WIDTH],
                            (((0,), (0,)), ((), ())), preferred_element_type=F32)
            for g in range(SSD_GROUPS)]
    state_ref[...] = state * chunk_decay_b + jnp.concatenate(made, axis=1)
    if not want_y:
        return None
    cm = act_ref[rows, SSD_WIDTH + GN:]
    x_bf = (xs * _dot(pk_dt_ref[rows, :], expand)).astype(BF16)
    state_bf = state.astype(BF16)
    a_cs_t = a_cs.T
    li = lax.broadcasted_iota(jnp.int32, (CHUNK, CHUNK), 0)
    si = lax.broadcasted_iota(jnp.int32, (CHUNK, CHUNK), 1)
    visible = (si <= li) if direction == 0 else (si >= li)
    slab_lane = lax.broadcasted_iota(jnp.int32, (CHUNK, KV_WIDTH), 1)
    heads_per_group = SSD_HEADS // SSD_GROUPS
    heads_per_slab = KV_WIDTH // SSD_HEAD_DIM
    slabs = []
    for g in range(SSD_GROUPS):
        cg = cm[:, g * D_STATE:(g + 1) * D_STATE]
        gcols = slice(g * GROUP_WIDTH, (g + 1) * GROUP_WIDTH)
        y_off = _dot(cg, state_bf[:, gcols]) * start_b[:, gcols]
        cb = cb_ref[rows, g * CHUNK:(g + 1) * CHUNK]
        for sl in range(GROUP_WIDTH // KV_WIDTH):
            slab0 = g * GROUP_WIDTH + sl * KV_WIDTH
            x_slab = x_bf[:, slab0:slab0 + KV_WIDTH]
            acc = y_off[:, sl * KV_WIDTH:(sl + 1) * KV_WIDTH]
            for pair in range(heads_per_slab // 2):
                mats, xheads = [], []
                for t in (2 * pair, 2 * pair + 1):
                    hd = g * heads_per_group + sl * heads_per_slab + t
                    lane = direction * SSD_HEADS + hd
                    seg = a_cs[:, lane:lane + 1] - a_cs_t[lane:lane + 1, :]
                    lmat = jnp.exp2(jnp.where(visible, seg, -jnp.inf))
                    mats.append((cb * lmat).astype(BF16))
                    in_head = (slab_lane >= t * SSD_HEAD_DIM) & (slab_lane < (t + 1) * SSD_HEAD_DIM)
                    xheads.append(jnp.where(in_head, x_slab, jnp.zeros_like(x_slab)))
                acc = acc + _dot(jnp.concatenate(mats, axis=1), jnp.concatenate(xheads, axis=0))
            slabs.append(acc)
    return jnp.concatenate(slabs, axis=1)


def _ssd_kernel(xl_ref, dtl_ref, z_ref, xc_ref, dtc_ref, cw_ref, cb_ref, alog_ref, dskip_ref, shift_ref, tri2_ref,
                e_fwd_ref, e_bwd_ref, y_ref, act_l, act_c, state_f, state_b,
                acs_l, pkdt_l, pkend_l, pkstart_l, acs_c, pkdt_c, pkend_c, pkstart_c, cbt_l, *, n_lat, n_ctx):
    aneg2 = -jnp.exp(alog_ref[...]) * LOG2E
    dt_lane = lax.broadcasted_iota(jnp.int32, (1, DT_PAD), 1)
    lane_ok = (dt_lane < 2 * SSD_HEADS).astype(F32)
    lane_fwd = dt_lane < SSD_HEADS
    piece_fwd = (dt_lane % PIECE_LANES) < SSD_HEADS
    terms_l = (acs_l, pkdt_l, pkend_l, pkstart_l)
    terms_c = (acs_c, pkdt_c, pkend_c, pkstart_c)

    def prepare(src_ref, dt_ref, act_ref, terms, cbt_ref, i, n_chunks):
        _conv_chunk(src_ref, act_ref, cw_ref, cb_ref, shift_ref, i, n_chunks)
        _chunk_terms(dt_ref, terms, i, aneg2, tri2_ref[...], lane_ok, lane_fwd, piece_fwd)
        if cbt_ref is not None:
            _chunk_cb(act_ref, cbt_ref, i)

    lax.fori_loop(0, n_lat, lambda i, c: (prepare(xl_ref, dtl_ref, act_l, terms_l, cbt_l, i, n_lat), c)[1], 0,
                  unroll=PREP_UNROLL)
    for i in range(n_ctx):
        prepare(xc_ref, dtc_ref, act_c, terms_c, None, i, n_ctx)
    expands = (e_fwd_ref[...], e_bwd_ref[...])
    states = (state_f, state_b)
    for direction in range(2):
        states[direction][...] = jnp.zeros_like(states[direction])
        for i in range(n_ctx):
            chunk = i if direction == 0 else n_ctx - 1 - i
            _ssd_chunk(act_c, terms_c, None, chunk * CHUNK, direction, False, expands[direction], states[direction])

    def step(i, second_half):
        for direction in range(2):
            chunk = i if direction == 0 else n_lat - 1 - i
            row0 = pl.multiple_of(chunk * CHUNK, CHUNK)
            y = _ssd_chunk(act_l, terms_l, cbt_l, row0, direction, True, expands[direction], states[direction])
            rows = pl.ds(row0, CHUNK)
            if second_half:
                xs = act_l[rows, :SSD_WIDTH].astype(F32)
                total = y_ref[0, rows, :].astype(F32) + y + dskip_ref[...] * xs
                y_ref[0, rows, :] = (total * z_ref[0, rows, :].astype(F32)).astype(BF16)
            else:
                y_ref[0, rows, :] = y.astype(BF16)

    half = n_lat // 2
    lax.fori_loop(0, half, lambda i, c: (step(i, False), c)[1], 0, unroll=SCAN_UNROLL)
    lax.fori_loop(half, n_lat, lambda i, c: (step(i, True), c)[1], 0, unroll=SCAN_UNROLL)


def _ssd_call(xbc_l, dt_l, z, xbc_c, dt_c, consts):
    bsz, rows, _ = xbc_l.shape
    lc = xbc_c.shape[1]
    per_batch = lambda shape: pl.BlockSpec((1,) + shape, lambda b: (b, 0, 0))
    assert rows % (2 * CHUNK) == 0 and lc % CHUNK == 0
    body = functools.partial(_ssd_kernel, n_lat=rows // CHUNK, n_ctx=lc // CHUNK)
    return pl.pallas_call(
        body,
        grid=(bsz,),
        in_specs=[per_batch((rows, CONV_CH)), per_batch((rows, DT_PAD)), per_batch((rows, SSD_WIDTH)),
                  per_batch((lc, CONV_CH)), per_batch((lc, DT_PAD)),
                  _const_spec((8, CONV_CH)), _const_spec((1, CONV_CH)), _const_spec((1, DT_PAD)),
                  _const_spec((1, SSD_WIDTH)), _const_spec(((D_CONV - 1) * CHUNK, CONV_WINDOW)),
                  _const_spec((2 * CHUNK, CHUNK)),
                  _const_spec((DT_PAD, SSD_WIDTH)), _const_spec((DT_PAD, SSD_WIDTH))],
        out_specs=per_batch((rows, SSD_WIDTH)),
        out_shape=jax.ShapeDtypeStruct((bsz, rows, SSD_WIDTH), BF16),
        scratch_shapes=[pltpu.VMEM((rows, CONV_CH), BF16), pltpu.VMEM((lc, CONV_CH), BF16),
                        pltpu.VMEM((D_STATE, SSD_WIDTH), F32), pltpu.VMEM((D_STATE, SSD_WIDTH), F32)]
                       + [pltpu.VMEM((n, DT_PAD), dt) for n in (rows, lc) for dt in (F32, BF16, BF16, BF16)]
                       + [pltpu.VMEM((rows, SSD_GROUPS * CHUNK), F32)],
        compiler_params=pltpu.CompilerParams(dimension_semantics=("arbitrary",),
                                             vmem_limit_bytes=VMEM_LIMIT),
        name="ssd_bidir",
    )(xbc_l, dt_l, z, xbc_c, dt_c, consts["conv_w"], consts["conv_b"], consts["a_log"], consts["d_skip"],
      consts["shift"], consts["tri2"], consts["e_fwd"], consts["e_bwd"])


def _out_kernel(ya_ref, t_ref, x_ref, mod_ref, wa_ref, ws_ref, sg_ref, fg_ref, out_ref):
    n_sub = ya_ref.shape[1] // OUT_SUB
    gate = mod_ref[0][:, 2 * D_MODEL:]

    def gated(c):
        r = slice(c * OUT_SUB, (c + 1) * OUT_SUB)
        t = t_ref[0, r, :].astype(F32)
        y_s = (t * lax.rsqrt(jnp.mean(t * t, axis=-1, keepdims=True) + EPS) * sg_ref[...]).astype(BF16)
        return ya_ref[0, r, :], y_s

    def project(c, y_a, y_s):
        r = slice(c * OUT_SUB, (c + 1) * OUT_SUB)
        new = x_ref[0, r, :] + gate * (_dot(y_a, wa_ref[...]) + _dot(y_s, ws_ref[...]))
        out_ref[0, r, :] = new * lax.rsqrt(jnp.mean(new * new, axis=-1, keepdims=True) + EPS) * fg_ref[...]

    ready = gated(0)
    for c in range(n_sub):
        upcoming = gated(c + 1) if c + 1 < n_sub else None
        project(c, *ready)
        ready = upcoming


def _out_call(y_attn, y_ssd, x, mod, w_a, w_s, ssd_gain, final_gain, *, tile):
    bsz, rows, _ = x.shape
    tok = pl.BlockSpec((1, tile, D_MODEL), lambda b, j: (b, j, 0))
    return pl.pallas_call(
        _out_kernel,
        grid=(bsz, rows // tile),
        in_specs=[tok, tok, tok,
                  pl.BlockSpec((1, 1, 3 * D_MODEL), lambda b, j: (b, 0, 0)),
                  _const_spec((ATTN_WIDTH, D_MODEL)), _const_spec((SSD_WIDTH, D_MODEL)),
                  _const_spec((1, SSD_WIDTH)), _const_spec((1, D_MODEL))],
        out_specs=tok,
        out_shape=jax.ShapeDtypeStruct((bsz, rows, D_MODEL), F32),
        compiler_params=pltpu.CompilerParams(dimension_semantics=("arbitrary", "arbitrary"),
                                             vmem_limit_bytes=VMEM_LIMIT),
        name="merge_out_proj",
    )(y_attn, y_ssd, x, mod, w_a, w_s, ssd_gain, final_gain)


def _rope_tables(rows):
    n_freq = HEAD_DIM // 4
    t = jnp.arange(rows, dtype=jnp.int32)
    pos = jnp.stack([(t // GRID_W).astype(F32), (t % GRID_W).astype(F32)], axis=1)
    inv_freq = ROPE_THETA ** (-jnp.arange(n_freq, dtype=F32) / n_freq)
    ang = pos[:, :, None] * inv_freq
    cos = jnp.cos(ang)[:, :, None, :]
    sin = jnp.sin(ang)[:, :, None, :]
    cos_h = jnp.broadcast_to(cos, (rows, 2, 2, n_freq)).reshape(rows, HEAD_DIM)
    sin_h = jnp.concatenate([-sin, sin], axis=2).reshape(rows, HEAD_DIM)
    reps = LANES // HEAD_DIM
    return jnp.tile(cos_h, (1, reps)), jnp.tile(sin_h, (1, reps))


def _pad_lanes(v, width):
    return jnp.pad(v, [(0, 0)] * (v.ndim - 1) + [(0, width - v.shape[-1])])


def kernel(x, c, ctx, c_ctx, ada_w, ada_b, norm_g, w_in, conv_w, conv_b, dt_bias, a_log, d_skip,
           q_norm_g, k_norm_g, ssd_norm_g, w_out, final_g):
    assert ada_w.shape[0] == 1, "single-layer problem: context outputs are never needed"
    bsz, rows, _ = x.shape
    ada_w, ada_b, norm_g, w_in, conv_w, conv_b = ada_w[0], ada_b[0], norm_g[0], w_in[0], conv_w[0], conv_b[0]
    dt_bias, a_log, d_skip = dt_bias[0], a_log[0], d_skip[0]
    q_norm_g, k_norm_g, ssd_norm_g, w_out = q_norm_g[0], k_norm_g[0], ssd_norm_g[0], w_out[0]

    n_rows = -(-(bsz + 1) // 8) * 8
    cc = jnp.concatenate([c, c_ctx[None, :], jnp.zeros((n_rows - bsz - 1, D_MODEL), F32)], axis=0)
    mod = _mod_call(cc, ada_w, ada_b)
    mod_l = mod[:bsz].reshape(bsz, 1, 3 * D_MODEL)
    mod_c = mod[bsz:bsz + 1].reshape(1, 1, 3 * D_MODEL)

    w_bf = w_in.astype(BF16)
    weights = {
        "k": w_bf[:, :OFF_V], "v": w_bf[:, OFF_V:OFF_XBC], "xbc": w_bf[:, OFF_XBC:OFF_DT],
        "dt": _pad_lanes(w_bf[:, OFF_DT:CTX_COLS], DT_PAD),
        "q": w_bf[:, OFF_Q:OFF_GA], "ga": w_bf[:, OFF_GA:OFF_Z], "z": w_bf[:, OFF_Z:],
    }
    cos, sin = _rope_tables(rows)
    head_of_lane = np.arange(KV_WIDTH) // HEAD_DIM
    consts = {
        "dt_bias": _pad_lanes(dt_bias.reshape(1, 2 * SSD_HEADS), DT_PAD),
        "k_gain": jnp.tile(k_norm_g, KV_WIDTH // HEAD_DIM).reshape(1, KV_WIDTH),
        "q_gain": jnp.tile(q_norm_g, KV_WIDTH // HEAD_DIM).reshape(1, KV_WIDTH),
        "bd": jnp.asarray(head_of_lane[:, None] == head_of_lane[None, :], BF16),
        "cos": cos, "sin": sin,
    }
    ng = norm_g.reshape(1, D_MODEL)
    vt_c, k_c, xbc_c, dt_c = _in_proj_call(ctx, mod_c, ng, weights, consts, latent=False,
                                           tile=min(IN_TILE, ctx.shape[1]))
    vt_l, k_l, xbc_l, dt_l, q, ga, z = _in_proj_call(x, mod_l, ng, weights, consts, latent=True, tile=IN_TILE)

    y_attn = _attn_call(q, ga, k_l, k_c, vt_l, vt_c, tile=ATTN_TILE)

    tok = np.arange(CHUNK)
    lane_head = np.arange(SSD_WIDTH) // SSD_HEAD_DIM
    dt_lane = np.arange(DT_PAD)
    ssd_consts = {
        "conv_w": jnp.pad(conv_w, ((0, 8 - D_CONV), (0, 0))),
        "conv_b": conv_b.reshape(1, CONV_CH),
        "a_log": _pad_lanes(a_log.reshape(1, 2 * SSD_HEADS), DT_PAD),
        "d_skip": jnp.repeat(d_skip, SSD_HEAD_DIM).reshape(1, SSD_WIDTH),
        "shift": jnp.asarray(np.concatenate(
            [np.arange(CONV_WINDOW)[None, :] == (CONV_HALO + tok[:, None] + k - CONV_PAD)
             for k in range(D_CONV) if k != CONV_PAD], axis=0), BF16),
        "tri2": jnp.asarray(np.concatenate([tok[None, :] <= tok[:, None], tok[None, :] >= tok[:, None]], axis=0), BF16),
        "e_fwd": jnp.asarray((dt_lane[:, None] % PIECE_LANES == lane_head[None, :])
                             & (dt_lane[:, None] < 3 * PIECE_LANES), BF16),
        "e_bwd": jnp.asarray((dt_lane[:, None] % PIECE_LANES == lane_head[None, :] + SSD_HEADS)
                             & (dt_lane[:, None] < 3 * PIECE_LANES), BF16),
    }
    y_ssd = _ssd_call(xbc_l, dt_l, z, xbc_c, dt_c, ssd_consts)

    w_out_bf = w_out.astype(BF16)
    return _out_call(y_attn, y_ssd, x, mod_l, w_out_bf[:ATTN_WIDTH], w_out_bf[ATTN_WIDTH:],
                     ssd_norm_g.reshape(1, SSD_WIDTH), final_g.reshape(1, D_MODEL), tile=OUT_TILE)
```

```python
import functools

import jax
import jax.numpy as jnp
import numpy as np
from jax import lax
from jax.experimental import pallas as pl
from jax.experimental.pallas import tpu as pltpu

F32 = jnp.float32
BF16 = jnp.bfloat16

D_MODEL = 1024
GRID_W = 64
HEAD_DIM = 64
N_Q_HEADS = 16
N_KV_HEADS = 4
Q_PER_KV = N_Q_HEADS // N_KV_HEADS
ATTN_WIDTH = N_Q_HEADS * HEAD_DIM
KV_WIDTH = N_KV_HEADS * HEAD_DIM
SSD_HEADS = 16
SSD_HEAD_DIM = 64
SSD_WIDTH = SSD_HEADS * SSD_HEAD_DIM
SSD_GROUPS = 2
GROUP_WIDTH = SSD_WIDTH // SSD_GROUPS
D_STATE = 128
GN = SSD_GROUPS * D_STATE
D_CONV = 5
CONV_PAD = D_CONV // 2
CONV_CH = SSD_WIDTH + 2 * GN
CHUNK = 128
ROPE_THETA = 10000.0
ATTN_SCALE = HEAD_DIM ** -0.5
EPS = 1e-6
OFF_V = KV_WIDTH
OFF_XBC = 2 * KV_WIDTH
OFF_DT = OFF_XBC + CONV_CH
CTX_COLS = OFF_DT + 2 * SSD_HEADS
OFF_Q = CTX_COLS
OFF_GA = OFF_Q + ATTN_WIDTH
OFF_Z = OFF_GA + ATTN_WIDTH

LANES = 128
DT_PAD = LANES
CONV_HALO = 16
CONV_WINDOW = 256
PIECE_LANES = 2 * SSD_HEADS
LOG2E = 1.4426950408889634
VMEM_LIMIT = 56 * 1024 * 1024

IN_TILE = 1024
OUT_TILE = 1024
IN_SUB = 1024
OUT_SUB = 256
ATTN_TILE = 256
KEY_BLOCK = 256
LOOKAHEAD = 4
HEADS_PER_PASS = 1
PREP_UNROLL = 4
SCAN_UNROLL = 2


def _silu(v):
    return v * (1.0 / (1.0 + jnp.exp2(v * (-LOG2E))))


def _softplus(v):
    return jnp.maximum(v, 0.0) + jnp.log1p(jnp.exp(-jnp.abs(v)))


def _split3(v):
    p1 = v.astype(BF16)
    r1 = v - p1.astype(F32)
    p2 = r1.astype(BF16)
    p3 = (r1 - p2.astype(F32)).astype(BF16)
    return p1, p2, p3


def _dot(a, b):
    return jnp.dot(a, b, preferred_element_type=F32)


def _const_spec(shape):
    nd = len(shape)
    return pl.BlockSpec(shape, lambda *_: (0,) * nd, pipeline_mode=pl.Buffered(1))


def _mod_kernel(c_ref, w_ref, b_ref, o_ref):
    s = _silu(c_ref[...])
    o_ref[...] = jnp.dot(s, w_ref[...], preferred_element_type=F32,
                         precision=lax.Precision.HIGHEST) + b_ref[...]


def _mod_call(cc, ada_w, ada_b):
    rows = cc.shape[0]
    n = ada_w.shape[1]
    bn = D_MODEL
    return pl.pallas_call(
        _mod_kernel,
        grid=(n // bn,),
        in_specs=[pl.BlockSpec((rows, D_MODEL), lambda j: (0, 0)),
                  pl.BlockSpec((D_MODEL, bn), lambda j: (0, j)),
                  pl.BlockSpec((1, bn), lambda j: (0, j))],
        out_specs=pl.BlockSpec((rows, bn), lambda j: (0, j)),
        out_shape=jax.ShapeDtypeStruct((rows, n), F32),
        compiler_params=pltpu.CompilerParams(dimension_semantics=("arbitrary",),
                                             vmem_limit_bytes=VMEM_LIMIT),
        name="adaln_mod",
    )(cc, ada_w, ada_b.reshape(1, n))


def _modulated_norm(x, mod, ng):
    ms = jnp.mean(x * x, axis=-1, keepdims=True)
    y = (x * lax.rsqrt(ms + EPS)) * ng
    return y * (1.0 + mod[:, D_MODEL:2 * D_MODEL]) + mod[:, :D_MODEL]


def _head_norm(v, gain, bd):
    ssq = _dot((v * v).astype(BF16), bd)
    return v * lax.rsqrt(ssq * (1.0 / HEAD_DIM) + EPS) * gain


def _rope(v, cos, sin_signed):
    lane = lax.broadcasted_iota(jnp.int32, v.shape, 1)
    up = pltpu.roll(v, LANES - 16, axis=1)
    down = pltpu.roll(v, 16, axis=1)
    partner = jnp.where((lane & 16) == 0, up, down)
    return v * cos + partner * sin_signed


def _in_kernel(x_ref, mod_ref, ng_ref, wk_ref, wv_ref, wx_ref, wdt_ref, dtb_ref, kg_ref, bd_ref, *rest,
               latent, tile):
    if latent:
        wq_ref, wg_ref, wz_ref, qg_ref, cos_ref, sin_ref, vt_out, k_out, xbc_out, dt_out, q_out, ga_out, z_out = rest
        tile_row0 = pl.multiple_of(pl.program_id(1) * tile, tile)
    else:
        vt_out, k_out, xbc_out, dt_out = rest
    sub = min(IN_SUB, tile)

    def normed(c):
        return _modulated_norm(x_ref[0, c * sub:(c + 1) * sub, :], mod_ref[0], ng_ref[...]).astype(BF16)

    def rotated(v, cos, sin):
        if not latent:
            return v
        return jnp.concatenate([_rope(v[:, t * LANES:(t + 1) * LANES], cos, sin)
                                for t in range(KV_WIDTH // LANES)], axis=1)

    def project(c, h):
        r = slice(c * sub, (c + 1) * sub)
        cos = sin = None
        if latent:
            cos = cos_ref[pl.ds(tile_row0 + c * sub, sub), :]
            sin = sin_ref[pl.ds(tile_row0 + c * sub, sub), :]
            for s in range(ATTN_WIDTH // KV_WIDTH):
                cols = slice(s * KV_WIDTH, (s + 1) * KV_WIDTH)
                qn = _head_norm(_dot(h, wq_ref[:, cols]), qg_ref[...], bd_ref[...])
                q_out[0, r, cols] = (rotated(qn, cos, sin) * (ATTN_SCALE * LOG2E)).astype(BF16)
        kn = _head_norm(_dot(h, wk_ref[...]), kg_ref[...], bd_ref[...])
        k_out[0, r, :] = rotated(kn, cos, sin).astype(BF16)
        xbc_out[0, r, :] = _dot(h, wx_ref[...]).astype(BF16)
        if latent:
            ga_out[0, r, :] = _silu(_dot(h, wg_ref[...])).astype(BF16)
            z_out[0, r, :] = _silu(_dot(h, wz_ref[...])).astype(BF16)
        vt_out[0, :, r] = _dot(h, wv_ref[...]).T.astype(BF16)
        dt_raw = _dot(h, wdt_ref[...]) + dtb_ref[...]
        lane = lax.broadcasted_iota(jnp.int32, dt_raw.shape, 1)
        dt_out[0, r, :] = jnp.where(lane < 2 * SSD_HEADS, _softplus(dt_raw), 0.0)

    ready = normed(0)
    for c in range(tile // sub):
        upcoming = normed(c + 1) if (c + 1) * sub < tile else None
        project(c, ready)
        ready = upcoming


def _in_proj_call(x, mod, ng, w, consts, *, latent, tile):
    bsz, rows, _ = x.shape
    nt = rows // tile
    tok = lambda width: pl.BlockSpec((1, tile, width), lambda b, j: (b, j, 0))
    mod_map = (lambda b, j: (b, 0, 0)) if latent else (lambda b, j: (0, 0, 0))
    in_specs = [tok(D_MODEL),
                pl.BlockSpec((1, 1, 3 * D_MODEL), mod_map),
                _const_spec((1, D_MODEL)),
                _const_spec((D_MODEL, KV_WIDTH)), _const_spec((D_MODEL, KV_WIDTH)),
                _const_spec((D_MODEL, CONV_CH)), _const_spec((D_MODEL, DT_PAD)),
                _const_spec((1, DT_PAD)), _const_spec((1, KV_WIDTH)), _const_spec((KV_WIDTH, KV_WIDTH))]
    args = [x, mod, ng, w["k"], w["v"], w["xbc"], w["dt"], consts["dt_bias"], consts["k_gain"], consts["bd"]]
    out_specs = [pl.BlockSpec((1, KV_WIDTH, tile), lambda b, j: (b, 0, j)),
                 tok(KV_WIDTH), tok(CONV_CH), tok(DT_PAD)]
    out_shape = [jax.ShapeDtypeStruct((bsz, KV_WIDTH, rows), BF16),
                 jax.ShapeDtypeStruct((bsz, rows, KV_WIDTH), BF16),
                 jax.ShapeDtypeStruct((bsz, rows, CONV_CH), BF16),
                 jax.ShapeDtypeStruct((bsz, rows, DT_PAD), F32)]
    if latent:
        in_specs += [_const_spec((D_MODEL, ATTN_WIDTH)), _const_spec((D_MODEL, ATTN_WIDTH)),
                     _const_spec((D_MODEL, SSD_WIDTH)), _const_spec((1, KV_WIDTH)),
                     _const_spec((rows, LANES)), _const_spec((rows, LANES))]
        args += [w["q"], w["ga"], w["z"], consts["q_gain"], consts["cos"], consts["sin"]]
        out_specs += [tok(ATTN_WIDTH), tok(ATTN_WIDTH), tok(SSD_WIDTH)]
        out_shape += [jax.ShapeDtypeStruct((bsz, rows, ATTN_WIDTH), BF16),
                      jax.ShapeDtypeStruct((bsz, rows, ATTN_WIDTH), BF16),
                      jax.ShapeDtypeStruct((bsz, rows, SSD_WIDTH), BF16)]
    body = functools.partial(_in_kernel, latent=latent, tile=tile)
    return pl.pallas_call(
        body,
        grid=(bsz, nt),
        in_specs=in_specs,
        out_specs=out_specs,
        out_shape=out_shape,
        compiler_params=pltpu.CompilerParams(dimension_semantics=("arbitrary", "arbitrary"),
                                             vmem_limit_bytes=VMEM_LIMIT),
        name="in_proj_latent" if latent else "in_proj_context",
    )(*args)


def _attn_kernel(q_ref, ga_ref, kl_ref, kc_ref, vtl_ref, vtc_ref, o_ref):
    tile = q_ref.shape[1]
    rows = kl_ref.shape[1]
    n_lat = rows // KEY_BLOCK
    n_blocks = n_lat + kc_ref.shape[1] // KEY_BLOCK
    n_pass = N_Q_HEADS // HEADS_PER_PASS
    q_t = q_ref[0].T
    zeros = jnp.zeros((HEAD_DIM, tile), BF16)

    def key_block(ref, transposed, kv, j):
        src, j = (ref[0], j) if j < n_lat else (ref[1], j - n_lat)
        if transposed:
            return src[0, kv * HEAD_DIM:(kv + 1) * HEAD_DIM, j * KEY_BLOCK:(j + 1) * KEY_BLOCK]
        return src[0, j * KEY_BLOCK:(j + 1) * KEY_BLOCK, :]

    q_pads = {}

    def scores(pass_idx, j):
        kv = pass_idx * HEADS_PER_PASS // Q_PER_KV
        if pass_idx not in q_pads:
            q_pads[pass_idx] = jnp.concatenate(
                [jnp.concatenate([zeros] * kv + [q_t[hq * HEAD_DIM:(hq + 1) * HEAD_DIM, :]]
                                 + [zeros] * (N_KV_HEADS - 1 - kv), axis=0)
                 for hq in range(pass_idx * HEADS_PER_PASS, (pass_idx + 1) * HEADS_PER_PASS)], axis=1)
        return _dot(key_block((kl_ref, kc_ref), False, kv, j), q_pads[pass_idx])

    def absorb(pass_idx, j, s, state):
        kv = pass_idx * HEADS_PER_PASS // Q_PER_KV
        block_max = jnp.max(s, axis=0, keepdims=True)
        if state is None:
            m_new = block_max
        else:
            m_old, l_old, acc_old = state
            m_new = jnp.maximum(m_old, block_max)
            alpha = jnp.exp2(m_old - m_new)
        p = jnp.exp2(s - m_new)
        l_new = jnp.sum(p, axis=0, keepdims=True)
        acc = _dot(key_block((vtl_ref, vtc_ref), True, kv, j), p.astype(BF16))
        if state is not None:
            l_new = l_new + alpha * l_old
            acc = acc + alpha * acc_old
        return m_new, l_new, acc

    stages = [(p_, j) for p_ in range(n_pass) for j in range(n_blocks)]
    outs = []
    state = None
    pending = [scores(*stages[i]) for i in range(LOOKAHEAD)]
    for i, (pass_idx, j) in enumerate(stages):
        if i + LOOKAHEAD < len(stages):
            pending.append(scores(*stages[i + LOOKAHEAD]))
        state = absorb(pass_idx, j, pending.pop(0), state)
        if j == n_blocks - 1:
            _, l_fin, acc = state
            o_t = acc * (1.0 / l_fin)
            outs += [o_t[:, t * tile:(t + 1) * tile] for t in range(HEADS_PER_PASS)]
            state = None
    o_ref[0] = jnp.concatenate(outs, axis=0).T.astype(BF16) * ga_ref[0]


def _attn_call(q, ga, k_l, k_c, vt_l, vt_c, *, tile):
    bsz, rows, _ = q.shape
    lc = k_c.shape[1]
    per_batch = lambda shape: pl.BlockSpec((1,) + shape, lambda b, j: (b, 0, 0))
    tok = pl.BlockSpec((1, tile, ATTN_WIDTH), lambda b, j: (b, j, 0))
    return pl.pallas_call(
        _attn_kernel,
        grid=(bsz, rows // tile),
        in_specs=[tok, tok,
                  per_batch((rows, KV_WIDTH)), per_batch((lc, KV_WIDTH)),
                  per_batch((KV_WIDTH, rows)), per_batch((KV_WIDTH, lc))],
        out_specs=pl.BlockSpec((1, tile, ATTN_WIDTH), lambda b, j: (b, j, 0)),
        out_shape=jax.ShapeDtypeStruct((bsz, rows, ATTN_WIDTH), BF16),
        compiler_params=pltpu.CompilerParams(dimension_semantics=("arbitrary", "arbitrary"),
                                             vmem_limit_bytes=VMEM_LIMIT),
        name="gqa_attention",
    )(q, ga, k_l, k_c, vt_l, vt_c)


def _conv_chunk(src_ref, dst_ref, cw_ref, cb_ref, shift_ref, i, n_chunks):
    i = jnp.asarray(i, jnp.int32)
    row0 = pl.multiple_of(i * CHUNK, CHUNK)
    main = src_ref[0, pl.ds(row0, CHUNK), :]
    prev0 = pl.multiple_of(jnp.maximum(row0 - CONV_HALO, 0), CONV_HALO)
    next0 = pl.multiple_of(jnp.minimum(row0 + CHUNK, (n_chunks - 1) * CHUNK + CHUNK - CONV_HALO), CONV_HALO)
    prev = src_ref[0, pl.ds(prev0, CONV_HALO), :] * (i > 0).astype(BF16)
    nxt = src_ref[0, pl.ds(next0, CONV_HALO), :] * (i < n_chunks - 1).astype(BF16)
    fill = jnp.zeros((CONV_WINDOW - CHUNK - 2 * CONV_HALO, CONV_CH), BF16)
    win = jnp.concatenate([prev, main, nxt, fill], axis=0)
    off_taps = [k for k in range(D_CONV) if k != CONV_PAD]
    for c0 in range(0, CONV_CH, GROUP_WIDTH):
        cols = slice(c0, c0 + GROUP_WIDTH)
        shifted = _dot(shift_ref[...], win[:, cols])
        acc = cb_ref[:, cols] + cw_ref[CONV_PAD:CONV_PAD + 1, cols] * main[:, cols].astype(F32)
        for j, k in enumerate(off_taps):
            acc = acc + cw_ref[k:k + 1, cols] * shifted[j * CHUNK:(j + 1) * CHUNK, :]
        dst_ref[pl.ds(row0, CHUNK), cols] = _silu(acc).astype(BF16)


def _pack3(v):
    p1, p2, p3 = _split3(v)
    packed = p1.astype(F32) + pltpu.roll(p2.astype(F32), PIECE_LANES, axis=1) \
        + pltpu.roll(p3.astype(F32), 2 * PIECE_LANES, axis=1)
    return packed.astype(BF16)


def _chunk_terms(dt_ref, terms, i, aneg2, tri2, lane_ok, lane_fwd, piece_fwd):
    acs_ref, pk_dt_ref, pk_end_ref, pk_start_ref = terms
    rows = pl.ds(pl.multiple_of(jnp.asarray(i, jnp.int32) * CHUNK, CHUNK), CHUNK)
    dtc = dt_ref[0, rows, :]
    r = _dot(tri2, _pack3(dtc * aneg2))
    r = jnp.where(piece_fwd, r[:CHUNK], r[CHUNK:])
    a_cs = (r + pltpu.roll(r, LANES - PIECE_LANES, axis=1) + pltpu.roll(r, LANES - 2 * PIECE_LANES, axis=1)) * lane_ok
    total = jnp.where(lane_fwd, a_cs[CHUNK - 1:CHUNK, :], a_cs[0:1, :])
    acs_ref[rows, :] = a_cs
    pk_dt_ref[rows, :] = _pack3(dtc)
    pk_end_ref[rows, :] = _pack3(dtc * jnp.exp2(total - a_cs) * lane_ok)
    pk_start_ref[rows, :] = _pack3(jnp.exp2(a_cs) * lane_ok)


def _chunk_cb(act_ref, cb_ref, i):
    rows = pl.ds(pl.multiple_of(jnp.asarray(i, jnp.int32) * CHUNK, CHUNK), CHUNK)
    for g in range(SSD_GROUPS):
        bg = act_ref[rows, SSD_WIDTH + g * D_STATE:SSD_WIDTH + (g + 1) * D_STATE]
        cg = act_ref[rows, SSD_WIDTH + GN + g * D_STATE:SSD_WIDTH + GN + (g + 1) * D_STATE]
        cb_ref[rows, g * CHUNK:(g + 1) * CHUNK] = lax.dot_general(
            cg, bg, (((1,), (1,)), ((), ())), preferred_element_type=F32)


def _ssd_chunk(act_ref, terms, cb_ref, row0, direction, want_y, expand, state_ref):
    acs_ref, pk_dt_ref, pk_end_ref, pk_start_ref = terms
    rows = pl.ds(row0, CHUNK)
    last = CHUNK - 1 if direction == 0 else 0
    a_cs = acs_ref[rows, :]
    dt_end_b = _dot(pk_end_ref[rows, :], expand)
    start_b = _dot(pk_start_ref[rows, :], expand)
    chunk_decay_b = start_b[last:last + 1, :]
    xs = act_ref[rows, :SSD_WIDTH].astype(F32)
    bm = act_ref[rows, SSD_WIDTH:SSD_WIDTH + GN]
    x_end = (xs * dt_end_b).astype(BF16)
    state = state_ref[...]
    made = [lax.dot_general(bm[:, g * D_STATE:(g + 1) * D_STATE],
                            x_end[:, g * GROUP_WIDTH:(g + 1) * GROUP_WIDTH],
                            (((0,), (0,)), ((), ())), preferred_element_type=F32)
            for g in range(SSD_GROUPS)]
    state_ref[...] = state * chunk_decay_b + jnp.concatenate(made, axis=1)
    if not want_y:
        return None
    cm = act_ref[rows, SSD_WIDTH + GN:]
    x_bf = (xs * _dot(pk_dt_ref[rows, :], expand)).astype(BF16)
    state_bf = state.astype(BF16)
    a_cs_t = a_cs.T
    li = lax.broadcasted_iota(jnp.int32, (CHUNK, CHUNK), 0)
    si = lax.broadcasted_iota(jnp.int32, (CHUNK, CHUNK), 1)
    visible = (si <= li) if direction == 0 else (si >= li)
    slab_lane = lax.broadcasted_iota(jnp.int32, (CHUNK, KV_WIDTH), 1)
    heads_per_group = SSD_HEADS // SSD_GROUPS
    heads_per_slab = KV_WIDTH // SSD_HEAD_DIM
    slabs = []
    for g in range(SSD_GROUPS):
        cg = cm[:, g * D_STATE:(g + 1) * D_STATE]
        gcols = slice(g * GROUP_WIDTH, (g + 1) * GROUP_WIDTH)
        y_off = _dot(cg, state_bf[:, gcols]) * start_b[:, gcols]
        cb = cb_ref[rows, g * CHUNK:(g + 1) * CHUNK]
        for sl in range(GROUP_WIDTH // KV_WIDTH):
            slab0 = g * GROUP_WIDTH + sl * KV_WIDTH
            x_slab = x_bf[:, slab0:slab0 + KV_WIDTH]
            acc = y_off[:, sl * KV_WIDTH:(sl + 1) * KV_WIDTH]
            for pair in range(heads_per_slab // 2):
                mats, xheads = [], []
                for t in (2 * pair, 2 * pair + 1):
                    hd = g * heads_per_group + sl * heads_per_slab + t
                    lane = direction * SSD_HEADS + hd
                    seg = a_cs[:, lane:lane + 1] - a_cs_t[lane:lane + 1, :]
                    lmat = jnp.exp2(jnp.where(visible, seg, -jnp.inf))
                    mats.append((cb * lmat).astype(BF16))
                    in_head = (slab_lane >= t * SSD_HEAD_DIM) & (slab_lane < (t + 1) * SSD_HEAD_DIM)
                    xheads.append(jnp.where(in_head, x_slab, jnp.zeros_like(x_slab)))
                acc = acc + _dot(jnp.concatenate(mats, axis=1), jnp.concatenate(xheads, axis=0))
            slabs.append(acc)
    return jnp.concatenate(slabs, axis=1)


def _ssd_kernel(xl_ref, dtl_ref, xc_ref, dtc_ref, cw_ref, cb_ref, alog_ref, dskip_ref, shift_ref, tri2_ref,
                e_fwd_ref, e_bwd_ref, y_ref, act_l, act_c, y_part, state_f, state_b,
                acs_l, pkdt_l, pkend_l, pkstart_l, acs_c, pkdt_c, pkend_c, pkstart_c, cbt_l, *, n_lat, n_ctx):
    aneg2 = -jnp.exp(alog_ref[...]) * LOG2E
    dt_lane = lax.broadcasted_iota(jnp.int32, (1, DT_PAD), 1)
    lane_ok = (dt_lane < 2 * SSD_HEADS).astype(F32)
    lane_fwd = dt_lane < SSD_HEADS
    piece_fwd = (dt_lane % PIECE_LANES) < SSD_HEADS
    terms_l = (acs_l, pkdt_l, pkend_l, pkstart_l)
    terms_c = (acs_c, pkdt_c, pkend_c, pkstart_c)

    def prepare(src_ref, dt_ref, act_ref, terms, cbt_ref, i, n_chunks):
        _conv_chunk(src_ref, act_ref, cw_ref, cb_ref, shift_ref, i, n_chunks)
        _chunk_terms(dt_ref, terms, i, aneg2, tri2_ref[...], lane_ok, lane_fwd, piece_fwd)
        if cbt_ref is not None:
            _chunk_cb(act_ref, cbt_ref, i)

    lax.fori_loop(0, n_lat, lambda i, c: (prepare(xl_ref, dtl_ref, act_l, terms_l, cbt_l, i, n_lat), c)[1], 0,
                  unroll=PREP_UNROLL)
    for i in range(n_ctx):
        prepare(xc_ref, dtc_ref, act_c, terms_c, None, i, n_ctx)
    expands = (e_fwd_ref[...], e_bwd_ref[...])
    states = (state_f, state_b)
    for direction in range(2):
        states[direction][...] = jnp.zeros_like(states[direction])
        for i in range(n_ctx):
            chunk = i if direction == 0 else n_ctx - 1 - i
            _ssd_chunk(act_c, terms_c, None, chunk * CHUNK, direction, False, expands[direction], states[direction])

    def step(i, second_half):
        for direction in range(2):
            chunk = i if direction == 0 else n_lat - 1 - i
            row0 = pl.multiple_of(chunk * CHUNK, CHUNK)
            y = _ssd_chunk(act_l, terms_l, cbt_l, row0, direction, True, expands[direction], states[direction])
            rows = pl.ds(row0, CHUNK)
            if second_half:
                xs = act_l[rows, :SSD_WIDTH].astype(F32)
                y_ref[0, rows, :] = (y_part[rows, :] + y + dskip_ref[...] * xs).astype(BF16)
            else:
                y_part[rows, :] = y

    half = n_lat // 2
    lax.fori_loop(0, half, lambda i, c: (step(i, False), c)[1], 0, unroll=SCAN_UNROLL)
    lax.fori_loop(half, n_lat, lambda i, c: (step(i, True), c)[1], 0, unroll=SCAN_UNROLL)


def _ssd_call(xbc_l, dt_l, xbc_c, dt_c, consts):
    bsz, rows, _ = xbc_l.shape
    lc = xbc_c.shape[1]
    per_batch = lambda shape: pl.BlockSpec((1,) + shape, lambda b: (b, 0, 0))
    assert rows % (2 * CHUNK) == 0 and lc % CHUNK == 0
    body = functools.partial(_ssd_kernel, n_lat=rows // CHUNK, n_ctx=lc // CHUNK)
    return pl.pallas_call(
        body,
        grid=(bsz,),
        in_specs=[per_batch((rows, CONV_CH)), per_batch((rows, DT_PAD)),
                  per_batch((lc, CONV_CH)), per_batch((lc, DT_PAD)),
                  _const_spec((8, CONV_CH)), _const_spec((1, CONV_CH)), _const_spec((1, DT_PAD)),
                  _const_spec((1, SSD_WIDTH)), _const_spec(((D_CONV - 1) * CHUNK, CONV_WINDOW)),
                  _const_spec((2 * CHUNK, CHUNK)),
                  _const_spec((DT_PAD, SSD_WIDTH)), _const_spec((DT_PAD, SSD_WIDTH))],
        out_specs=per_batch((rows, SSD_WIDTH)),
        out_shape=jax.ShapeDtypeStruct((bsz, rows, SSD_WIDTH), BF16),
        scratch_shapes=[pltpu.VMEM((rows, CONV_CH), BF16), pltpu.VMEM((lc, CONV_CH), BF16),
                        pltpu.VMEM((rows, SSD_WIDTH), F32),
                        pltpu.VMEM((D_STATE, SSD_WIDTH), F32), pltpu.VMEM((D_STATE, SSD_WIDTH), F32)]
                       + [pltpu.VMEM((n, DT_PAD), dt) for n in (rows, lc) for dt in (F32, BF16, BF16, BF16)]
                       + [pltpu.VMEM((rows, SSD_GROUPS * CHUNK), F32)],
        compiler_params=pltpu.CompilerParams(dimension_semantics=("arbitrary",),
                                             vmem_limit_bytes=VMEM_LIMIT),
        name="ssd_bidir",
    )(xbc_l, dt_l, xbc_c, dt_c, consts["conv_w"], consts["conv_b"], consts["a_log"], consts["d_skip"],
      consts["shift"], consts["tri2"], consts["e_fwd"], consts["e_bwd"])


def _out_kernel(ya_ref, y_ref, z_ref, x_ref, mod_ref, wa_ref, ws_ref, sg_ref, fg_ref, out_ref):
    n_sub = ya_ref.shape[1] // OUT_SUB
    gate = mod_ref[0][:, 2 * D_MODEL:]

    def gated(c):
        r = slice(c * OUT_SUB, (c + 1) * OUT_SUB)
        t = y_ref[0, r, :].astype(F32) * z_ref[0, r, :].astype(F32)
        y_s = (t * lax.rsqrt(jnp.mean(t * t, axis=-1, keepdims=True) + EPS) * sg_ref[...]).astype(BF16)
        return ya_ref[0, r, :], y_s

    def project(c, y_a, y_s):
        r = slice(c * OUT_SUB, (c + 1) * OUT_SUB)
        new = x_ref[0, r, :] + gate * (_dot(y_a, wa_ref[...]) + _dot(y_s, ws_ref[...]))
        out_ref[0, r, :] = new * lax.rsqrt(jnp.mean(new * new, axis=-1, keepdims=True) + EPS) * fg_ref[...]

    ready = gated(0)
    for c in range(n_sub):
        upcoming = gated(c + 1) if c + 1 < n_sub else None
        project(c, *ready)
        ready = upcoming


def _out_call(y_attn, y_ssd, z, x, mod, w_a, w_s, ssd_gain, final_gain, *, tile):
    bsz, rows, _ = x.shape
    tok = pl.BlockSpec((1, tile, D_MODEL), lambda b, j: (b, j, 0))
    return pl.pallas_call(
        _out_kernel,
        grid=(bsz, rows // tile),
        in_specs=[tok, tok, tok, tok,
                  pl.BlockSpec((1, 1, 3 * D_MODEL), lambda b, j: (b, 0, 0)),
                  _const_spec((ATTN_WIDTH, D_MODEL)), _const_spec((SSD_WIDTH, D_MODEL)),
                  _const_spec((1, SSD_WIDTH)), _const_spec((1, D_MODEL))],
        out_specs=tok,
        out_shape=jax.ShapeDtypeStruct((bsz, rows, D_MODEL), F32),
        compiler_params=pltpu.CompilerParams(dimension_semantics=("arbitrary", "arbitrary"),
                                             vmem_limit_bytes=VMEM_LIMIT),
        name="merge_out_proj",
    )(y_attn, y_ssd, z, x, mod, w_a, w_s, ssd_gain, final_gain)


def _rope_tables(rows):
    n_freq = HEAD_DIM // 4
    t = jnp.arange(rows, dtype=jnp.int32)
    pos = jnp.stack([(t // GRID_W).astype(F32), (t % GRID_W).astype(F32)], axis=1)
    inv_freq = ROPE_THETA ** (-jnp.arange(n_freq, dtype=F32) / n_freq)
    ang = pos[:, :, None] * inv_freq
    cos = jnp.cos(ang)[:, :, None, :]
    sin = jnp.sin(ang)[:, :, None, :]
    cos_h = jnp.broadcast_to(cos, (rows, 2, 2, n_freq)).reshape(rows, HEAD_DIM)
    sin_h = jnp.concatenate([-sin, sin], axis=2).reshape(rows, HEAD_DIM)
    reps = LANES // HEAD_DIM
    return jnp.tile(cos_h, (1, reps)), jnp.tile(sin_h, (1, reps))


def _pad_lanes(v, width):
    return jnp.pad(v, [(0, 0)] * (v.ndim - 1) + [(0, width - v.shape[-1])])


def kernel(x, c, ctx, c_ctx, ada_w, ada_b, norm_g, w_in, conv_w, conv_b, dt_bias, a_log, d_skip,
           q_norm_g, k_norm_g, ssd_norm_g, w_out, final_g):
    assert ada_w.shape[0] == 1, "single-layer problem: context outputs are never needed"
    bsz, rows, _ = x.shape
    ada_w, ada_b, norm_g, w_in, conv_w, conv_b = ada_w[0], ada_b[0], norm_g[0], w_in[0], conv_w[0], conv_b[0]
    dt_bias, a_log, d_skip = dt_bias[0], a_log[0], d_skip[0]
    q_norm_g, k_norm_g, ssd_norm_g, w_out = q_norm_g[0], k_norm_g[0], ssd_norm_g[0], w_out[0]

    n_rows = -(-(bsz + 1) // 8) * 8
    cc = jnp.concatenate([c, c_ctx[None, :], jnp.zeros((n_rows - bsz - 1, D_MODEL), F32)], axis=0)
    mod = _mod_call(cc, ada_w, ada_b)
    mod_l = mod[:bsz].reshape(bsz, 1, 3 * D_MODEL)
    mod_c = mod[bsz:bsz + 1].reshape(1, 1, 3 * D_MODEL)

    w_bf = w_in.astype(BF16)
    weights = {
        "k": w_bf[:, :OFF_V], "v": w_bf[:, OFF_V:OFF_XBC], "xbc": w_bf[:, OFF_XBC:OFF_DT],
        "dt": _pad_lanes(w_bf[:, OFF_DT:CTX_COLS], DT_PAD),
        "q": w_bf[:, OFF_Q:OFF_GA], "ga": w_bf[:, OFF_GA:OFF_Z], "z": w_bf[:, OFF_Z:],
    }
    cos, sin = _rope_tables(rows)
    head_of_lane = np.arange(KV_WIDTH) // HEAD_DIM
    consts = {
        "dt_bias": _pad_lanes(dt_bias.reshape(1, 2 * SSD_HEADS), DT_PAD),
        "k_gain": jnp.tile(k_norm_g, KV_WIDTH // HEAD_DIM).reshape(1, KV_WIDTH),
        "q_gain": jnp.tile(q_norm_g, KV_WIDTH // HEAD_DIM).reshape(1, KV_WIDTH),
        "bd": jnp.asarray(head_of_lane[:, None] == head_of_lane[None, :], BF16),
        "cos": cos, "sin": sin,
    }
    ng = norm_g.reshape(1, D_MODEL)
    vt_c, k_c, xbc_c, dt_c = _in_proj_call(ctx, mod_c, ng, weights, consts, latent=False,
                                           tile=min(IN_TILE, ctx.shape[1]))
    vt_l, k_l, xbc_l, dt_l, q, ga, z = _in_proj_call(x, mod_l, ng, weights, consts, latent=True, tile=IN_TILE)

    y_attn = _attn_call(q, ga, k_l, k_c, vt_l, vt_c, tile=ATTN_TILE)

    tok = np.arange(CHUNK)
    lane_head = np.arange(SSD_WIDTH) // SSD_HEAD_DIM
    dt_lane = np.arange(DT_PAD)
    ssd_consts = {
        "conv_w": jnp.pad(conv_w, ((0, 8 - D_CONV), (0, 0))),
        "conv_b": conv_b.reshape(1, CONV_CH),
        "a_log": _pad_lanes(a_log.reshape(1, 2 * SSD_HEADS), DT_PAD),
        "d_skip": jnp.repeat(d_skip, SSD_HEAD_DIM).reshape(1, SSD_WIDTH),
        "shift": jnp.asarray(np.concatenate(
            [np.arange(CONV_WINDOW)[None, :] == (CONV_HALO + tok[:, None] + k - CONV_PAD)
             for k in range(D_CONV) if k != CONV_PAD], axis=0), BF16),
        "tri2": jnp.asarray(np.concatenate([tok[None, :] <= tok[:, None], tok[None, :] >= tok[:, None]], axis=0), BF16),
        "e_fwd": jnp.asarray((dt_lane[:, None] % PIECE_LANES == lane_head[None, :])
                             & (dt_lane[:, None] < 3 * PIECE_LANES), BF16),
        "e_bwd": jnp.asarray((dt_lane[:, None] % PIECE_LANES == lane_head[None, :] + SSD_HEADS)
                             & (dt_lane[:, None] < 3 * PIECE_LANES), BF16),
    }
    y_ssd = _ssd_call(xbc_l, dt_l, xbc_c, dt_c, ssd_consts)

    w_out_bf = w_out.astype(BF16)
    return _out_call(y_attn, y_ssd, z, x, mod_l, w_out_bf[:ATTN_WIDTH], w_out_bf[ATTN_WIDTH:],
                     ssd_norm_g.reshape(1, SSD_WIDTH), final_g.reshape(1, D_MODEL), tile=OUT_TILE)
```

```python
import functools

import jax
import jax.numpy as jnp
import numpy as np
from jax import lax
from jax.experimental import pallas as pl
from jax.experimental.pallas import tpu as pltpu

F32 = jnp.float32
BF16 = jnp.bfloat16

D_MODEL = 1024
GRID_W = 64
HEAD_DIM = 64
N_Q_HEADS = 16
N_KV_HEADS = 4
Q_PER_KV = N_Q_HEADS // N_KV_HEADS
ATTN_WIDTH = N_Q_HEADS * HEAD_DIM
KV_WIDTH = N_KV_HEADS * HEAD_DIM
SSD_HEADS = 16
SSD_HEAD_DIM = 64
SSD_WIDTH = SSD_HEADS * SSD_HEAD_DIM
SSD_GROUPS = 2
GROUP_WIDTH = SSD_WIDTH // SSD_GROUPS
D_STATE = 128
GN = SSD_GROUPS * D_STATE
D_CONV = 5
CONV_PAD = D_CONV // 2
CONV_CH = SSD_WIDTH + 2 * GN
CHUNK = 128
ROPE_THETA = 10000.0
ATTN_SCALE = HEAD_DIM ** -0.5
EPS = 1e-6
OFF_V = KV_WIDTH
OFF_XBC = 2 * KV_WIDTH
OFF_DT = OFF_XBC + CONV_CH
CTX_COLS = OFF_DT + 2 * SSD_HEADS
OFF_Q = CTX_COLS
OFF_GA = OFF_Q + ATTN_WIDTH
OFF_Z = OFF_GA + ATTN_WIDTH

LANES = 128
DT_PAD = LANES
CONV_HALO = 16
CONV_WINDOW = 256
PIECE_LANES = 2 * SSD_HEADS
LOG2E = 1.4426950408889634
VMEM_LIMIT = 56 * 1024 * 1024

IN_TILE = 1024
OUT_TILE = 1024
IN_SUB = 1024
OUT_SUB = 256
ATTN_TILE = 256
KEY_BLOCK = 256
LOOKAHEAD = 3
HEADS_PER_PASS = 2
PREP_UNROLL = 4
SCAN_UNROLL = 2


def _silu(v):
    return v * (1.0 / (1.0 + jnp.exp2(v * (-LOG2E))))


def _softplus(v):
    return jnp.maximum(v, 0.0) + jnp.log1p(jnp.exp(-jnp.abs(v)))


def _split3(v):
    p1 = v.astype(BF16)
    r1 = v - p1.astype(F32)
    p2 = r1.astype(BF16)
    p3 = (r1 - p2.astype(F32)).astype(BF16)
    return p1, p2, p3


def _dot(a, b):
    return jnp.dot(a, b, preferred_element_type=F32)


def _const_spec(shape):
    nd = len(shape)
    return pl.BlockSpec(shape, lambda *_: (0,) * nd, pipeline_mode=pl.Buffered(1))


def _mod_kernel(c_ref, w_ref, b_ref, o_ref):
    s = _silu(c_ref[...])
    o_ref[...] = jnp.dot(s, w_ref[...], preferred_element_type=F32,
                         precision=lax.Precision.HIGHEST) + b_ref[...]


def _mod_call(cc, ada_w, ada_b):
    rows = cc.shape[0]
    n = ada_w.shape[1]
    bn = D_MODEL
    return pl.pallas_call(
        _mod_kernel,
        grid=(n // bn,),
        in_specs=[pl.BlockSpec((rows, D_MODEL), lambda j: (0, 0)),
                  pl.BlockSpec((D_MODEL, bn), lambda j: (0, j)),
                  pl.BlockSpec((1, bn), lambda j: (0, j))],
        out_specs=pl.BlockSpec((rows, bn), lambda j: (0, j)),
        out_shape=jax.ShapeDtypeStruct((rows, n), F32),
        compiler_params=pltpu.CompilerParams(dimension_semantics=("arbitrary",),
                                             vmem_limit_bytes=VMEM_LIMIT),
        name="adaln_mod",
    )(cc, ada_w, ada_b.reshape(1, n))


def _modulated_norm(x, mod, ng):
    ms = jnp.mean(x * x, axis=-1, keepdims=True)
    y = (x * lax.rsqrt(ms + EPS)) * ng
    return y * (1.0 + mod[:, D_MODEL:2 * D_MODEL]) + mod[:, :D_MODEL]


def _head_norm(v, gain, bd):
    ssq = _dot((v * v).astype(BF16), bd)
    return v * lax.rsqrt(ssq * (1.0 / HEAD_DIM) + EPS) * gain


def _rope(v, cos, sin_signed):
    lane = lax.broadcasted_iota(jnp.int32, v.shape, 1)
    up = pltpu.roll(v, LANES - 16, axis=1)
    down = pltpu.roll(v, 16, axis=1)
    partner = jnp.where((lane & 16) == 0, up, down)
    return v * cos + partner * sin_signed


def _in_kernel(x_ref, mod_ref, ng_ref, wk_ref, wv_ref, wx_ref, wdt_ref, dtb_ref, kg_ref, bd_ref, *rest,
               latent, tile):
    if latent:
        wq_ref, wg_ref, wz_ref, qg_ref, cos_ref, sin_ref, vt_out, k_out, xbc_out, dt_out, q_out, ga_out, z_out = rest
        tile_row0 = pl.multiple_of(pl.program_id(1) * tile, tile)
    else:
        vt_out, k_out, xbc_out, dt_out = rest
    sub = min(IN_SUB, tile)

    def normed(c):
        return _modulated_norm(x_ref[0, c * sub:(c + 1) * sub, :], mod_ref[0], ng_ref[...]).astype(BF16)

    def rotated(v, cos, sin):
        if not latent:
            return v
        return jnp.concatenate([_rope(v[:, t * LANES:(t + 1) * LANES], cos, sin)
                                for t in range(KV_WIDTH // LANES)], axis=1)

    def project(c, h):
        r = slice(c * sub, (c + 1) * sub)
        cos = sin = None
        if latent:
            cos = cos_ref[pl.ds(tile_row0 + c * sub, sub), :]
            sin = sin_ref[pl.ds(tile_row0 + c * sub, sub), :]
            for s in range(ATTN_WIDTH // KV_WIDTH):
                cols = slice(s * KV_WIDTH, (s + 1) * KV_WIDTH)
                qn = _head_norm(_dot(h, wq_ref[:, cols]), qg_ref[...], bd_ref[...])
                q_out[0, r, cols] = (rotated(qn, cos, sin) * (ATTN_SCALE * LOG2E)).astype(BF16)
        kn = _head_norm(_dot(h, wk_ref[...]), kg_ref[...], bd_ref[...])
        k_out[0, r, :] = rotated(kn, cos, sin).astype(BF16)
        xbc_out[0, r, :] = _dot(h, wx_ref[...]).astype(BF16)
        if latent:
            ga_out[0, r, :] = _silu(_dot(h, wg_ref[...])).astype(BF16)
            z_out[0, r, :] = _silu(_dot(h, wz_ref[...])).astype(BF16)
        vt_out[0, :, r] = _dot(h, wv_ref[...]).T.astype(BF16)
        dt_raw = _dot(h, wdt_ref[...]) + dtb_ref[...]
        lane = lax.broadcasted_iota(jnp.int32, dt_raw.shape, 1)
        dt_out[0, r, :] = jnp.where(lane < 2 * SSD_HEADS, _softplus(dt_raw), 0.0)

    ready = normed(0)
    for c in range(tile // sub):
        upcoming = normed(c + 1) if (c + 1) * sub < tile else None
        project(c, ready)
        ready = upcoming


def _in_proj_call(x, mod, ng, w, consts, *, latent, tile):
    bsz, rows, _ = x.shape
    nt = rows // tile
    tok = lambda width: pl.BlockSpec((1, tile, width), lambda b, j: (b, j, 0))
    mod_map = (lambda b, j: (b, 0, 0)) if latent else (lambda b, j: (0, 0, 0))
    in_specs = [tok(D_MODEL),
                pl.BlockSpec((1, 1, 3 * D_MODEL), mod_map),
                _const_spec((1, D_MODEL)),
                _const_spec((D_MODEL, KV_WIDTH)), _const_spec((D_MODEL, KV_WIDTH)),
                _const_spec((D_MODEL, CONV_CH)), _const_spec((D_MODEL, DT_PAD)),
                _const_spec((1, DT_PAD)), _const_spec((1, KV_WIDTH)), _const_spec((KV_WIDTH, KV_WIDTH))]
    args = [x, mod, ng, w["k"], w["v"], w["xbc"], w["dt"], consts["dt_bias"], consts["k_gain"], consts["bd"]]
    out_specs = [pl.BlockSpec((1, KV_WIDTH, tile), lambda b, j: (b, 0, j)),
                 tok(KV_WIDTH), tok(CONV_CH), tok(DT_PAD)]
    out_shape = [jax.ShapeDtypeStruct((bsz, KV_WIDTH, rows), BF16),
                 jax.ShapeDtypeStruct((bsz, rows, KV_WIDTH), BF16),
                 jax.ShapeDtypeStruct((bsz, rows, CONV_CH), BF16),
                 jax.ShapeDtypeStruct((bsz, rows, DT_PAD), F32)]
    if latent:
        in_specs += [_const_spec((D_MODEL, ATTN_WIDTH)), _const_spec((D_MODEL, ATTN_WIDTH)),
                     _const_spec((D_MODEL, SSD_WIDTH)), _const_spec((1, KV_WIDTH)),
                     _const_spec((rows, LANES)), _const_spec((rows, LANES))]
        args += [w["q"], w["ga"], w["z"], consts["q_gain"], consts["cos"], consts["sin"]]
        out_specs += [tok(ATTN_WIDTH), tok(ATTN_WIDTH), tok(SSD_WIDTH)]
        out_shape += [jax.ShapeDtypeStruct((bsz, rows, ATTN_WIDTH), BF16),
                      jax.ShapeDtypeStruct((bsz, rows, ATTN_WIDTH), BF16),
                      jax.ShapeDtypeStruct((bsz, rows, SSD_WIDTH), BF16)]
    body = functools.partial(_in_kernel, latent=latent, tile=tile)
    return pl.pallas_call(
        body,
        grid=(bsz, nt),
        in_specs=in_specs,
        out_specs=out_specs,
        out_shape=out_shape,
        compiler_params=pltpu.CompilerParams(dimension_semantics=("arbitrary", "arbitrary"),
                                             vmem_limit_bytes=VMEM_LIMIT),
        name="in_proj_latent" if latent else "in_proj_context",
    )(*args)


def _attn_kernel(q_ref, ga_ref, kl_ref, kc_ref, vtl_ref, vtc_ref, o_ref):
    tile = q_ref.shape[1]
    rows = kl_ref.shape[1]
    n_lat = rows // KEY_BLOCK
    n_blocks = n_lat + kc_ref.shape[1] // KEY_BLOCK
    n_pass = N_Q_HEADS // HEADS_PER_PASS
    q_t = q_ref[0].T
    zeros = jnp.zeros((HEAD_DIM, tile), BF16)

    def key_block(ref, transposed, kv, j):
        src, j = (ref[0], j) if j < n_lat else (ref[1], j - n_lat)
        if transposed:
            return src[0, kv * HEAD_DIM:(kv + 1) * HEAD_DIM, j * KEY_BLOCK:(j + 1) * KEY_BLOCK]
        return src[0, j * KEY_BLOCK:(j + 1) * KEY_BLOCK, :]

    q_pads = {}

    def scores(pass_idx, j):
        kv = pass_idx * HEADS_PER_PASS // Q_PER_KV
        if pass_idx not in q_pads:
            q_pads[pass_idx] = jnp.concatenate(
                [jnp.concatenate([zeros] * kv + [q_t[hq * HEAD_DIM:(hq + 1) * HEAD_DIM, :]]
                                 + [zeros] * (N_KV_HEADS - 1 - kv), axis=0)
                 for hq in range(pass_idx * HEADS_PER_PASS, (pass_idx + 1) * HEADS_PER_PASS)], axis=1)
        return _dot(key_block((kl_ref, kc_ref), False, kv, j), q_pads[pass_idx])

    def absorb(pass_idx, j, s, state):
        kv = pass_idx * HEADS_PER_PASS // Q_PER_KV
        block_max = jnp.max(s, axis=0, keepdims=True)
        if state is None:
            m_new = block_max
        else:
            m_old, l_old, acc_old = state
            m_new = jnp.maximum(m_old, block_max)
            alpha = jnp.exp2(m_old - m_new)
        p = jnp.exp2(s - m_new)
        l_new = jnp.sum(p, axis=0, keepdims=True)
        acc = _dot(key_block((vtl_ref, vtc_ref), True, kv, j), p.astype(BF16))
        if state is not None:
            l_new = l_new + alpha * l_old
            acc = acc + alpha * acc_old
        return m_new, l_new, acc

    stages = [(p_, j) for p_ in range(n_pass) for j in range(n_blocks)]
    outs = []
    state = None
    pending = [scores(*stages[i]) for i in range(LOOKAHEAD)]
    for i, (pass_idx, j) in enumerate(stages):
        if i + LOOKAHEAD < len(stages):
            pending.append(scores(*stages[i + LOOKAHEAD]))
        state = absorb(pass_idx, j, pending.pop(0), state)
        if j == n_blocks - 1:
            _, l_fin, acc = state
            o_t = acc * (1.0 / l_fin)
            outs += [o_t[:, t * tile:(t + 1) * tile] for t in range(HEADS_PER_PASS)]
            state = None
    o_ref[0] = jnp.concatenate(outs, axis=0).T.astype(BF16) * ga_ref[0]


def _attn_call(q, ga, k_l, k_c, vt_l, vt_c, *, tile):
    bsz, rows, _ = q.shape
    lc = k_c.shape[1]
    per_batch = lambda shape: pl.BlockSpec((1,) + shape, lambda b, j: (b, 0, 0))
    tok = pl.BlockSpec((1, tile, ATTN_WIDTH), lambda b, j: (b, j, 0))
    return pl.pallas_call(
        _attn_kernel,
        grid=(bsz, rows // tile),
        in_specs=[tok, tok,
                  per_batch((rows, KV_WIDTH)), per_batch((lc, KV_WIDTH)),
                  per_batch((KV_WIDTH, rows)), per_batch((KV_WIDTH, lc))],
        out_specs=pl.BlockSpec((1, tile, ATTN_WIDTH), lambda b, j: (b, j, 0)),
        out_shape=jax.ShapeDtypeStruct((bsz, rows, ATTN_WIDTH), BF16),
        compiler_params=pltpu.CompilerParams(dimension_semantics=("arbitrary", "arbitrary"),
                                             vmem_limit_bytes=VMEM_LIMIT),
        name="gqa_attention",
    )(q, ga, k_l, k_c, vt_l, vt_c)


def _conv_chunk(src_ref, dst_ref, cw_ref, cb_ref, shift_ref, i, n_chunks):
    i = jnp.asarray(i, jnp.int32)
    row0 = pl.multiple_of(i * CHUNK, CHUNK)
    main = src_ref[0, pl.ds(row0, CHUNK), :]
    prev0 = pl.multiple_of(jnp.maximum(row0 - CONV_HALO, 0), CONV_HALO)
    next0 = pl.multiple_of(jnp.minimum(row0 + CHUNK, (n_chunks - 1) * CHUNK + CHUNK - CONV_HALO), CONV_HALO)
    prev = src_ref[0, pl.ds(prev0, CONV_HALO), :] * (i > 0).astype(BF16)
    nxt = src_ref[0, pl.ds(next0, CONV_HALO), :] * (i < n_chunks - 1).astype(BF16)
    fill = jnp.zeros((CONV_WINDOW - CHUNK - 2 * CONV_HALO, CONV_CH), BF16)
    win = jnp.concatenate([prev, main, nxt, fill], axis=0)
    off_taps = [k for k in range(D_CONV) if k != CONV_PAD]
    for c0 in range(0, CONV_CH, GROUP_WIDTH):
        cols = slice(c0, c0 + GROUP_WIDTH)
        shifted = _dot(shift_ref[...], win[:, cols])
        acc = cb_ref[:, cols] + cw_ref[CONV_PAD:CONV_PAD + 1, cols] * main[:, cols].astype(F32)
        for j, k in enumerate(off_taps):
            acc = acc + cw_ref[k:k + 1, cols] * shifted[j * CHUNK:(j + 1) * CHUNK, :]
        dst_ref[pl.ds(row0, CHUNK), cols] = _silu(acc).astype(BF16)


def _pack3(v):
    p1, p2, p3 = _split3(v)
    packed = p1.astype(F32) + pltpu.roll(p2.astype(F32), PIECE_LANES, axis=1) \
        + pltpu.roll(p3.astype(F32), 2 * PIECE_LANES, axis=1)
    return packed.astype(BF16)


def _chunk_terms(dt_ref, terms, i, aneg2, tri2, lane_ok, lane_fwd, piece_fwd):
    acs_ref, src_t_ref, pk_end_ref, pk_start_ref = terms
    rows = pl.ds(pl.multiple_of(jnp.asarray(i, jnp.int32) * CHUNK, CHUNK), CHUNK)
    dtc = dt_ref[0, rows, :]
    r = _dot(tri2, _pack3(dtc * aneg2))
    r = jnp.where(piece_fwd, r[:CHUNK], r[CHUNK:])
    a_cs = (r + pltpu.roll(r, LANES - PIECE_LANES, axis=1) + pltpu.roll(r, LANES - 2 * PIECE_LANES, axis=1)) * lane_ok
    total = jnp.where(lane_fwd, a_cs[CHUNK - 1:CHUNK, :], a_cs[0:1, :])
    acs_ref[rows, :] = a_cs
    src_t_ref[rows, :] = (jnp.log2(dtc) - a_cs).T
    pk_end_ref[rows, :] = _pack3(dtc * jnp.exp2(total - a_cs) * lane_ok)
    pk_start_ref[rows, :] = _pack3(jnp.exp2(a_cs) * lane_ok)


def _chunk_cb(act_ref, cb_ref, i):
    rows = pl.ds(pl.multiple_of(jnp.asarray(i, jnp.int32) * CHUNK, CHUNK), CHUNK)
    for g in range(SSD_GROUPS):
        bg = act_ref[rows, SSD_WIDTH + g * D_STATE:SSD_WIDTH + (g + 1) * D_STATE]
        cg = act_ref[rows, SSD_WIDTH + GN + g * D_STATE:SSD_WIDTH + GN + (g + 1) * D_STATE]
        cb_ref[rows, g * CHUNK:(g + 1) * CHUNK] = lax.dot_general(
            cg, bg, (((1,), (1,)), ((), ())), preferred_element_type=F32)


def _ssd_chunk(act_ref, terms, cb_ref, row0, direction, want_y, expand, state_ref):
    acs_ref, src_t_ref, pk_end_ref, pk_start_ref = terms
    rows = pl.ds(row0, CHUNK)
    last = CHUNK - 1 if direction == 0 else 0
    a_cs = acs_ref[rows, :]
    dt_end_b = _dot(pk_end_ref[rows, :], expand)
    start_b = _dot(pk_start_ref[rows, :], expand)
    chunk_decay_b = start_b[last:last + 1, :]
    xs = act_ref[rows, :SSD_WIDTH].astype(F32)
    bm = act_ref[rows, SSD_WIDTH:SSD_WIDTH + GN]
    x_end = (xs * dt_end_b).astype(BF16)
    state = state_ref[...]
    made = [lax.dot_general(bm[:, g * D_STATE:(g + 1) * D_STATE],
                            x_end[:, g * GROUP_WIDTH:(g + 1) * GROUP_WIDTH],
                            (((0,), (0,)), ((), ())), preferred_element_type=F32)
            for g in range(SSD_GROUPS)]
    state_ref[...] = state * chunk_decay_b + jnp.concatenate(made, axis=1)
    if not want_y:
        return None
    cm = act_ref[rows, SSD_WIDTH + GN:]
    state_bf = state.astype(BF16)
    src_t = src_t_ref[rows, :]
    li = lax.broadcasted_iota(jnp.int32, (CHUNK, CHUNK), 0)
    si = lax.broadcasted_iota(jnp.int32, (CHUNK, CHUNK), 1)
    visible = (si <= li) if direction == 0 else (si >= li)
    slab_lane = lax.broadcasted_iota(jnp.int32, (CHUNK, KV_WIDTH), 1)
    heads_per_group = SSD_HEADS // SSD_GROUPS
    heads_per_slab = KV_WIDTH // SSD_HEAD_DIM
    slabs = []
    for g in range(SSD_GROUPS):
        cg = cm[:, g * D_STATE:(g + 1) * D_STATE]
        gcols = slice(g * GROUP_WIDTH, (g + 1) * GROUP_WIDTH)
        y_off = _dot(cg, state_bf[:, gcols]) * start_b[:, gcols]
        cb = cb_ref[rows, g * CHUNK:(g + 1) * CHUNK]
        for sl in range(GROUP_WIDTH // KV_WIDTH):
            slab0 = g * GROUP_WIDTH + sl * KV_WIDTH
            x_slab = act_ref[rows, slab0:slab0 + KV_WIDTH]
            acc = y_off[:, sl * KV_WIDTH:(sl + 1) * KV_WIDTH]
            for pair in range(heads_per_slab // 2):
                mats, xheads = [], []
                for t in (2 * pair, 2 * pair + 1):
                    hd = g * heads_per_group + sl * heads_per_slab + t
                    lane = direction * SSD_HEADS + hd
                    seg = a_cs[:, lane:lane + 1] + src_t[lane:lane + 1, :]
                    lmat = jnp.exp2(jnp.where(visible, seg, -jnp.inf))
                    mats.append((cb * lmat).astype(BF16))
                    in_head = (slab_lane >= t * SSD_HEAD_DIM) & (slab_lane < (t + 1) * SSD_HEAD_DIM)
                    xheads.append(jnp.where(in_head, x_slab, jnp.zeros_like(x_slab)))
                acc = acc + _dot(jnp.concatenate(mats, axis=1), jnp.concatenate(xheads, axis=0))
            slabs.append(acc)
    return jnp.concatenate(slabs, axis=1)


def _ssd_kernel(xl_ref, dtl_ref, xc_ref, dtc_ref, cw_ref, cb_ref, alog_ref, dskip_ref, shift_ref, tri2_ref,
                e_fwd_ref, e_bwd_ref, y_ref, act_l, act_c, y_part, state_f, state_b,
                acs_l, srct_l, pkend_l, pkstart_l, acs_c, srct_c, pkend_c, pkstart_c, cbt_l, *, n_lat, n_ctx):
    aneg2 = -jnp.exp(alog_ref[...]) * LOG2E
    dt_lane = lax.broadcasted_iota(jnp.int32, (1, DT_PAD), 1)
    lane_ok = (dt_lane < 2 * SSD_HEADS).astype(F32)
    lane_fwd = dt_lane < SSD_HEADS
    piece_fwd = (dt_lane % PIECE_LANES) < SSD_HEADS
    terms_l = (acs_l, srct_l, pkend_l, pkstart_l)
    terms_c = (acs_c, srct_c, pkend_c, pkstart_c)

    def prepare(src_ref, dt_ref, act_ref, terms, cbt_ref, i, n_chunks):
        _conv_chunk(src_ref, act_ref, cw_ref, cb_ref, shift_ref, i, n_chunks)
        _chunk_terms(dt_ref, terms, i, aneg2, tri2_ref[...], lane_ok, lane_fwd, piece_fwd)
        if cbt_ref is not None:
            _chunk_cb(act_ref, cbt_ref, i)

    lax.fori_loop(0, n_lat, lambda i, c: (prepare(xl_ref, dtl_ref, act_l, terms_l, cbt_l, i, n_lat), c)[1], 0,
                  unroll=PREP_UNROLL)
    for i in range(n_ctx):
        prepare(xc_ref, dtc_ref, act_c, terms_c, None, i, n_ctx)
    expands = (e_fwd_ref[...], e_bwd_ref[...])
    states = (state_f, state_b)
    for direction in range(2):
        states[direction][...] = jnp.zeros_like(states[direction])
        for i in range(n_ctx):
            chunk = i if direction == 0 else n_ctx - 1 - i
            _ssd_chunk(act_c, terms_c, None, chunk * CHUNK, direction, False, expands[direction], states[direction])

    def step(i, second_half):
        for direction in range(2):
            chunk = i if direction == 0 else n_lat - 1 - i
            row0 = pl.multiple_of(chunk * CHUNK, CHUNK)
            y = _ssd_chunk(act_l, terms_l, cbt_l, row0, direction, True, expands[direction], states[direction])
            rows = pl.ds(row0, CHUNK)
            if second_half:
                xs = act_l[rows, :SSD_WIDTH].astype(F32)
                y_ref[0, rows, :] = (y_part[rows, :] + y + dskip_ref[...] * xs).astype(BF16)
            else:
                y_part[rows, :] = y

    half = n_lat // 2
    lax.fori_loop(0, half, lambda i, c: (step(i, False), c)[1], 0, unroll=SCAN_UNROLL)
    lax.fori_loop(half, n_lat, lambda i, c: (step(i, True), c)[1], 0, unroll=SCAN_UNROLL)


def _ssd_call(xbc_l, dt_l, xbc_c, dt_c, consts):
    bsz, rows, _ = xbc_l.shape
    lc = xbc_c.shape[1]
    per_batch = lambda shape: pl.BlockSpec((1,) + shape, lambda b: (b, 0, 0))
    assert rows % (2 * CHUNK) == 0 and lc % CHUNK == 0
    body = functools.partial(_ssd_kernel, n_lat=rows // CHUNK, n_ctx=lc // CHUNK)
    return pl.pallas_call(
        body,
        grid=(bsz,),
        in_specs=[per_batch((rows, CONV_CH)), per_batch((rows, DT_PAD)),
                  per_batch((lc, CONV_CH)), per_batch((lc, DT_PAD)),
                  _const_spec((8, CONV_CH)), _const_spec((1, CONV_CH)), _const_spec((1, DT_PAD)),
                  _const_spec((1, SSD_WIDTH)), _const_spec(((D_CONV - 1) * CHUNK, CONV_WINDOW)),
                  _const_spec((2 * CHUNK, CHUNK)),
                  _const_spec((DT_PAD, SSD_WIDTH)), _const_spec((DT_PAD, SSD_WIDTH))],
        out_specs=per_batch((rows, SSD_WIDTH)),
        out_shape=jax.ShapeDtypeStruct((bsz, rows, SSD_WIDTH), BF16),
        scratch_shapes=[pltpu.VMEM((rows, CONV_CH), BF16), pltpu.VMEM((lc, CONV_CH), BF16),
                        pltpu.VMEM((rows, SSD_WIDTH), F32),
                        pltpu.VMEM((D_STATE, SSD_WIDTH), F32), pltpu.VMEM((D_STATE, SSD_WIDTH), F32)]
                       + [pltpu.VMEM((n, DT_PAD), dt) for n in (rows, lc) for dt in (F32, F32, BF16, BF16)]
                       + [pltpu.VMEM((rows, SSD_GROUPS * CHUNK), F32)],
        compiler_params=pltpu.CompilerParams(dimension_semantics=("arbitrary",),
                                             vmem_limit_bytes=VMEM_LIMIT),
        name="ssd_bidir",
    )(xbc_l, dt_l, xbc_c, dt_c, consts["conv_w"], consts["conv_b"], consts["a_log"], consts["d_skip"],
      consts["shift"], consts["tri2"], consts["e_fwd"], consts["e_bwd"])


def _out_kernel(ya_ref, y_ref, z_ref, x_ref, mod_ref, wa_ref, ws_ref, sg_ref, fg_ref, out_ref):
    n_sub = ya_ref.shape[1] // OUT_SUB
    gate = mod_ref[0][:, 2 * D_MODEL:]

    def gated(c):
        r = slice(c * OUT_SUB, (c + 1) * OUT_SUB)
        t = y_ref[0, r, :].astype(F32) * z_ref[0, r, :].astype(F32)
        y_s = (t * lax.rsqrt(jnp.mean(t * t, axis=-1, keepdims=True) + EPS) * sg_ref[...]).astype(BF16)
        return ya_ref[0, r, :], y_s

    def project(c, y_a, y_s):
        r = slice(c * OUT_SUB, (c + 1) * OUT_SUB)
        new = x_ref[0, r, :] + gate * (_dot(y_a, wa_ref[...]) + _dot(y_s, ws_ref[...]))
        out_ref[0, r, :] = new * lax.rsqrt(jnp.mean(new * new, axis=-1, keepdims=True) + EPS) * fg_ref[...]

    ready = gated(0)
    for c in range(n_sub):
        upcoming = gated(c + 1) if c + 1 < n_sub else None
        project(c, *ready)
        ready = upcoming


def _out_call(y_attn, y_ssd, z, x, mod, w_a, w_s, ssd_gain, final_gain, *, tile):
    bsz, rows, _ = x.shape
    tok = pl.BlockSpec((1, tile, D_MODEL), lambda b, j: (b, j, 0))
    return pl.pallas_call(
        _out_kernel,
        grid=(bsz, rows // tile),
        in_specs=[tok, tok, tok, tok,
                  pl.BlockSpec((1, 1, 3 * D_MODEL), lambda b, j: (b, 0, 0)),
                  _const_spec((ATTN_WIDTH, D_MODEL)), _const_spec((SSD_WIDTH, D_MODEL)),
                  _const_spec((1, SSD_WIDTH)), _const_spec((1, D_MODEL))],
        out_specs=tok,
        out_shape=jax.ShapeDtypeStruct((bsz, rows, D_MODEL), F32),
        compiler_params=pltpu.CompilerParams(dimension_semantics=("arbitrary", "arbitrary"),
                                             vmem_limit_bytes=VMEM_LIMIT),
        name="merge_out_proj",
    )(y_attn, y_ssd, z, x, mod, w_a, w_s, ssd_gain, final_gain)


def _rope_tables(rows):
    n_freq = HEAD_DIM // 4
    t = jnp.arange(rows, dtype=jnp.int32)
    pos = jnp.stack([(t // GRID_W).astype(F32), (t % GRID_W).astype(F32)], axis=1)
    inv_freq = ROPE_THETA ** (-jnp.arange(n_freq, dtype=F32) / n_freq)
    ang = pos[:, :, None] * inv_freq
    cos = jnp.cos(ang)[:, :, None, :]
    sin = jnp.sin(ang)[:, :, None, :]
    cos_h = jnp.broadcast_to(cos, (rows, 2, 2, n_freq)).reshape(rows, HEAD_DIM)
    sin_h = jnp.concatenate([-sin, sin], axis=2).reshape(rows, HEAD_DIM)
    reps = LANES // HEAD_DIM
    return jnp.tile(cos_h, (1, reps)), jnp.tile(sin_h, (1, reps))


def _pad_lanes(v, width):
    return jnp.pad(v, [(0, 0)] * (v.ndim - 1) + [(0, width - v.shape[-1])])


def kernel(x, c, ctx, c_ctx, ada_w, ada_b, norm_g, w_in, conv_w, conv_b, dt_bias, a_log, d_skip,
           q_norm_g, k_norm_g, ssd_norm_g, w_out, final_g):
    assert ada_w.shape[0] == 1, "single-layer problem: context outputs are never needed"
    bsz, rows, _ = x.shape
    ada_w, ada_b, norm_g, w_in, conv_w, conv_b = ada_w[0], ada_b[0], norm_g[0], w_in[0], conv_w[0], conv_b[0]
    dt_bias, a_log, d_skip = dt_bias[0], a_log[0], d_skip[0]
    q_norm_g, k_norm_g, ssd_norm_g, w_out = q_norm_g[0], k_norm_g[0], ssd_norm_g[0], w_out[0]

    n_rows = -(-(bsz + 1) // 8) * 8
    cc = jnp.concatenate([c, c_ctx[None, :], jnp.zeros((n_rows - bsz - 1, D_MODEL), F32)], axis=0)
    mod = _mod_call(cc, ada_w, ada_b)
    mod_l = mod[:bsz].reshape(bsz, 1, 3 * D_MODEL)
    mod_c = mod[bsz:bsz + 1].reshape(1, 1, 3 * D_MODEL)

    w_bf = w_in.astype(BF16)
    weights = {
        "k": w_bf[:, :OFF_V], "v": w_bf[:, OFF_V:OFF_XBC], "xbc": w_bf[:, OFF_XBC:OFF_DT],
        "dt": _pad_lanes(w_bf[:, OFF_DT:CTX_COLS], DT_PAD),
        "q": w_bf[:, OFF_Q:OFF_GA], "ga": w_bf[:, OFF_GA:OFF_Z], "z": w_bf[:, OFF_Z:],
    }
    cos, sin = _rope_tables(rows)
    head_of_lane = np.arange(KV_WIDTH) // HEAD_DIM
    consts = {
        "dt_bias": _pad_lanes(dt_bias.reshape(1, 2 * SSD_HEADS), DT_PAD),
        "k_gain": jnp.tile(k_norm_g, KV_WIDTH // HEAD_DIM).reshape(1, KV_WIDTH),
        "q_gain": jnp.tile(q_norm_g, KV_WIDTH // HEAD_DIM).reshape(1, KV_WIDTH),
        "bd": jnp.asarray(head_of_lane[:, None] == head_of_lane[None, :], BF16),
        "cos": cos, "sin": sin,
    }
    ng = norm_g.reshape(1, D_MODEL)
    vt_c, k_c, xbc_c, dt_c = _in_proj_call(ctx, mod_c, ng, weights, consts, latent=False,
                                           tile=min(IN_TILE, ctx.shape[1]))
    vt_l, k_l, xbc_l, dt_l, q, ga, z = _in_proj_call(x, mod_l, ng, weights, consts, latent=True, tile=IN_TILE)

    y_attn = _attn_call(q, ga, k_l, k_c, vt_l, vt_c, tile=ATTN_TILE)

    tok = np.arange(CHUNK)
    lane_head = np.arange(SSD_WIDTH) // SSD_HEAD_DIM
    dt_lane = np.arange(DT_PAD)
    ssd_consts = {
        "conv_w": jnp.pad(conv_w, ((0, 8 - D_CONV), (0, 0))),
        "conv_b": conv_b.reshape(1, CONV_CH),
        "a_log": _pad_lanes(a_log.reshape(1, 2 * SSD_HEADS), DT_PAD),
        "d_skip": jnp.repeat(d_skip, SSD_HEAD_DIM).reshape(1, SSD_WIDTH),
        "shift": jnp.asarray(np.concatenate(
            [np.arange(CONV_WINDOW)[None, :] == (CONV_HALO + tok[:, None] + k - CONV_PAD)
             for k in range(D_CONV) if k != CONV_PAD], axis=0), BF16),
        "tri2": jnp.asarray(np.concatenate([tok[None, :] <= tok[:, None], tok[None, :] >= tok[:, None]], axis=0), BF16),
        "e_fwd": jnp.asarray((dt_lane[:, None] % PIECE_LANES == lane_head[None, :])
                             & (dt_lane[:, None] < 3 * PIECE_LANES), BF16),
        "e_bwd": jnp.asarray((dt_lane[:, None] % PIECE_LANES == lane_head[None, :] + SSD_HEADS)
                             & (dt_lane[:, None] < 3 * PIECE_LANES), BF16),
    }
    y_ssd = _ssd_call(xbc_l, dt_l, xbc_c, dt_c, ssd_consts)

    w_out_bf = w_out.astype(BF16)
    return _out_call(y_attn, y_ssd, z, x, mod_l, w_out_bf[:ATTN_WIDTH], w_out_bf[ATTN_WIDTH:],
                     ssd_norm_g.reshape(1, SSD_WIDTH), final_g.reshape(1, D_MODEL), tile=OUT_TILE)
```

```python
import functools

import jax
import jax.numpy as jnp
import numpy as np
from jax import lax
from jax.experimental import pallas as pl
from jax.experimental.pallas import tpu as pltpu

F32 = jnp.float32
BF16 = jnp.bfloat16

D_MODEL = 1024
GRID_W = 64
HEAD_DIM = 64
N_Q_HEADS = 16
N_KV_HEADS = 4
Q_PER_KV = N_Q_HEADS // N_KV_HEADS
ATTN_WIDTH = N_Q_HEADS * HEAD_DIM
KV_WIDTH = N_KV_HEADS * HEAD_DIM
SSD_HEADS = 16
SSD_HEAD_DIM = 64
SSD_WIDTH = SSD_HEADS * SSD_HEAD_DIM
SSD_GROUPS = 2
GROUP_WIDTH = SSD_WIDTH // SSD_GROUPS
D_STATE = 128
GN = SSD_GROUPS * D_STATE
D_CONV = 5
CONV_PAD = D_CONV // 2
CONV_CH = SSD_WIDTH + 2 * GN
CHUNK = 128
ROPE_THETA = 10000.0
ATTN_SCALE = HEAD_DIM ** -0.5
EPS = 1e-6
OFF_V = KV_WIDTH
OFF_XBC = 2 * KV_WIDTH
OFF_DT = OFF_XBC + CONV_CH
CTX_COLS = OFF_DT + 2 * SSD_HEADS
OFF_Q = CTX_COLS
OFF_GA = OFF_Q + ATTN_WIDTH
OFF_Z = OFF_GA + ATTN_WIDTH

LANES = 128
DT_PAD = LANES
CONV_HALO = 16
CONV_WINDOW = 256
PIECE_LANES = 2 * SSD_HEADS
LOG2E = 1.4426950408889634
VMEM_LIMIT = 56 * 1024 * 1024

IN_TILE = 1024
OUT_TILE = 1024
IN_SUB = 1024
OUT_SUB = 256
ATTN_TILE = 256
KEY_BLOCK = 256
LOOKAHEAD = 3
HEADS_PER_PASS = 2
PREP_UNROLL = 4
SCAN_UNROLL = 2


def _silu(v):
    return v * (1.0 / (1.0 + jnp.exp2(v * (-LOG2E))))


def _softplus(v):
    return jnp.maximum(v, 0.0) + jnp.log1p(jnp.exp(-jnp.abs(v)))


def _split3(v):
    p1 = v.astype(BF16)
    r1 = v - p1.astype(F32)
    p2 = r1.astype(BF16)
    p3 = (r1 - p2.astype(F32)).astype(BF16)
    return p1, p2, p3


def _dot(a, b):
    return jnp.dot(a, b, preferred_element_type=F32)


def _const_spec(shape):
    nd = len(shape)
    return pl.BlockSpec(shape, lambda *_: (0,) * nd, pipeline_mode=pl.Buffered(1))


def _mod_kernel(c_ref, w_ref, b_ref, o_ref):
    s = _silu(c_ref[...])
    o_ref[...] = jnp.dot(s, w_ref[...], preferred_element_type=F32,
                         precision=lax.Precision.HIGHEST) + b_ref[...]


def _mod_call(cc, ada_w, ada_b):
    rows = cc.shape[0]
    n = ada_w.shape[1]
    bn = D_MODEL
    return pl.pallas_call(
        _mod_kernel,
        grid=(n // bn,),
        in_specs=[pl.BlockSpec((rows, D_MODEL), lambda j: (0, 0)),
                  pl.BlockSpec((D_MODEL, bn), lambda j: (0, j)),
                  pl.BlockSpec((1, bn), lambda j: (0, j))],
        out_specs=pl.BlockSpec((rows, bn), lambda j: (0, j)),
        out_shape=jax.ShapeDtypeStruct((rows, n), F32),
        compiler_params=pltpu.CompilerParams(dimension_semantics=("arbitrary",),
                                             vmem_limit_bytes=VMEM_LIMIT),
        name="adaln_mod",
    )(cc, ada_w, ada_b.reshape(1, n))


def _modulated_norm(x, mod, ng):
    ms = jnp.mean(x * x, axis=-1, keepdims=True)
    y = (x * lax.rsqrt(ms + EPS)) * ng
    return y * (1.0 + mod[:, D_MODEL:2 * D_MODEL]) + mod[:, :D_MODEL]


def _head_norm(v, gain, bd):
    ssq = _dot((v * v).astype(BF16), bd)
    return v * lax.rsqrt(ssq * (1.0 / HEAD_DIM) + EPS) * gain


def _rope(v, cos, sin_signed):
    lane = lax.broadcasted_iota(jnp.int32, v.shape, 1)
    up = pltpu.roll(v, LANES - 16, axis=1)
    down = pltpu.roll(v, 16, axis=1)
    partner = jnp.where((lane & 16) == 0, up, down)
    return v * cos + partner * sin_signed


def _in_kernel(x_ref, mod_ref, ng_ref, wk_ref, wv_ref, wx_ref, wdt_ref, dtb_ref, kg_ref, bd_ref, *rest,
               latent, tile):
    if latent:
        wq_ref, wg_ref, wz_ref, qg_ref, cos_ref, sin_ref, vt_out, k_out, xbc_out, dt_out, q_out, ga_out, z_out = rest
        tile_row0 = pl.multiple_of(pl.program_id(1) * tile, tile)
    else:
        vt_out, k_out, xbc_out, dt_out = rest
    sub = min(IN_SUB, tile)

    def normed(c):
        return _modulated_norm(x_ref[0, c * sub:(c + 1) * sub, :], mod_ref[0], ng_ref[...]).astype(BF16)

    def rotated(v, cos, sin):
        if not latent:
            return v
        return jnp.concatenate([_rope(v[:, t * LANES:(t + 1) * LANES], cos, sin)
                                for t in range(KV_WIDTH // LANES)], axis=1)

    def project(c, h):
        r = slice(c * sub, (c + 1) * sub)
        cos = sin = None
        if latent:
            cos = cos_ref[pl.ds(tile_row0 + c * sub, sub), :]
            sin = sin_ref[pl.ds(tile_row0 + c * sub, sub), :]
            for s in range(ATTN_WIDTH // KV_WIDTH):
                cols = slice(s * KV_WIDTH, (s + 1) * KV_WIDTH)
                qn = _head_norm(_dot(h, wq_ref[:, cols]), qg_ref[...], bd_ref[...])
                q_out[0, r, cols] = (rotated(qn, cos, sin) * (ATTN_SCALE * LOG2E)).astype(BF16)
        kn = _head_norm(_dot(h, wk_ref[...]), kg_ref[...], bd_ref[...])
        k_out[0, r, :] = rotated(kn, cos, sin).astype(BF16)
        xbc_out[0, r, :] = _dot(h, wx_ref[...]).astype(BF16)
        if latent:
            ga_out[0, r, :] = _silu(_dot(h, wg_ref[...])).astype(BF16)
            z_out[0, r, :] = _silu(_dot(h, wz_ref[...])).astype(BF16)
        vt_out[0, :, r] = _dot(h, wv_ref[...]).T.astype(BF16)
        dt_raw = _dot(h, wdt_ref[...]) + dtb_ref[...]
        lane = lax.broadcasted_iota(jnp.int32, dt_raw.shape, 1)
        dt_out[0, r, :] = jnp.where(lane < 2 * SSD_HEADS, _softplus(dt_raw), 0.0)

    ready = normed(0)
    for c in range(tile // sub):
        upcoming = normed(c + 1) if (c + 1) * sub < tile else None
        project(c, ready)
        ready = upcoming


def _in_proj_call(x, mod, ng, w, consts, *, latent, tile):
    bsz, rows, _ = x.shape
    nt = rows // tile
    tok = lambda width: pl.BlockSpec((1, tile, width), lambda b, j: (b, j, 0))
    mod_map = (lambda b, j: (b, 0, 0)) if latent else (lambda b, j: (0, 0, 0))
    in_specs = [tok(D_MODEL),
                pl.BlockSpec((1, 1, 3 * D_MODEL), mod_map),
                _const_spec((1, D_MODEL)),
                _const_spec((D_MODEL, KV_WIDTH)), _const_spec((D_MODEL, KV_WIDTH)),
                _const_spec((D_MODEL, CONV_CH)), _const_spec((D_MODEL, DT_PAD)),
                _const_spec((1, DT_PAD)), _const_spec((1, KV_WIDTH)), _const_spec((KV_WIDTH, KV_WIDTH))]
    args = [x, mod, ng, w["k"], w["v"], w["xbc"], w["dt"], consts["dt_bias"], consts["k_gain"], consts["bd"]]
    out_specs = [pl.BlockSpec((1, KV_WIDTH, tile), lambda b, j: (b, 0, j)),
                 tok(KV_WIDTH), tok(CONV_CH), tok(DT_PAD)]
    out_shape = [jax.ShapeDtypeStruct((bsz, KV_WIDTH, rows), BF16),
                 jax.ShapeDtypeStruct((bsz, rows, KV_WIDTH), BF16),
                 jax.ShapeDtypeStruct((bsz, rows, CONV_CH), BF16),
                 jax.ShapeDtypeStruct((bsz, rows, DT_PAD), F32)]
    if latent:
        in_specs += [_const_spec((D_MODEL, ATTN_WIDTH)), _const_spec((D_MODEL, ATTN_WIDTH)),
                     _const_spec((D_MODEL, SSD_WIDTH)), _const_spec((1, KV_WIDTH)),
                     _const_spec((rows, LANES)), _const_spec((rows, LANES))]
        args += [w["q"], w["ga"], w["z"], consts["q_gain"], consts["cos"], consts["sin"]]
        out_specs += [tok(ATTN_WIDTH), tok(ATTN_WIDTH), tok(SSD_WIDTH)]
        out_shape += [jax.ShapeDtypeStruct((bsz, rows, ATTN_WIDTH), BF16),
                      jax.ShapeDtypeStruct((bsz, rows, ATTN_WIDTH), BF16),
                      jax.ShapeDtypeStruct((bsz, rows, SSD_WIDTH), BF16)]
    body = functools.partial(_in_kernel, latent=latent, tile=tile)
    return pl.pallas_call(
        body,
        grid=(bsz, nt),
        in_specs=in_specs,
        out_specs=out_specs,
        out_shape=out_shape,
        compiler_params=pltpu.CompilerParams(dimension_semantics=("arbitrary", "arbitrary"),
                                             vmem_limit_bytes=VMEM_LIMIT),
        name="in_proj_latent" if latent else "in_proj_context",
    )(*args)


def _attn_kernel(q_ref, ga_ref, kl_ref, kc_ref, vtl_ref, vtc_ref, o_ref):
    tile = q_ref.shape[1]
    rows = kl_ref.shape[1]
    blocks = [(0, j * KEY_BLOCK, KEY_BLOCK) for j in range(rows // KEY_BLOCK)] + [(1, 0, kc_ref.shape[1])]
    n_blocks = len(blocks)
    n_pass = N_Q_HEADS // HEADS_PER_PASS
    q_t = q_ref[0].T
    zeros = jnp.zeros((HEAD_DIM, tile), BF16)

    def key_block(ref, transposed, kv, j):
        which, start, size = blocks[j]
        if transposed:
            return ref[which][0, kv * HEAD_DIM:(kv + 1) * HEAD_DIM, start:start + size]
        return ref[which][0, start:start + size, :]

    q_pads = {}

    def scores(pass_idx, j):
        kv = pass_idx * HEADS_PER_PASS // Q_PER_KV
        if pass_idx not in q_pads:
            q_pads[pass_idx] = jnp.concatenate(
                [jnp.concatenate([zeros] * kv + [q_t[hq * HEAD_DIM:(hq + 1) * HEAD_DIM, :]]
                                 + [zeros] * (N_KV_HEADS - 1 - kv), axis=0)
                 for hq in range(pass_idx * HEADS_PER_PASS, (pass_idx + 1) * HEADS_PER_PASS)], axis=1)
        return _dot(key_block((kl_ref, kc_ref), False, kv, j), q_pads[pass_idx])

    def absorb(pass_idx, j, s, state):
        kv = pass_idx * HEADS_PER_PASS // Q_PER_KV
        block_max = jnp.max(s, axis=0, keepdims=True)
        if state is None:
            m_new = block_max
        else:
            m_old, l_old, acc_old = state
            m_new = jnp.maximum(m_old, block_max)
            alpha = jnp.exp2(m_old - m_new)
        p = jnp.exp2(s - m_new)
        l_new = jnp.sum(p, axis=0, keepdims=True)
        acc = _dot(key_block((vtl_ref, vtc_ref), True, kv, j), p.astype(BF16))
        if state is not None:
            l_new = l_new + alpha * l_old
            acc = acc + alpha * acc_old
        return m_new, l_new, acc

    stages = [(p_, j) for p_ in range(n_pass) for j in range(n_blocks)]
    outs = []
    state = None
    pending = [scores(*stages[i]) for i in range(LOOKAHEAD)]
    for i, (pass_idx, j) in enumerate(stages):
        if i + LOOKAHEAD < len(stages):
            pending.append(scores(*stages[i + LOOKAHEAD]))
        state = absorb(pass_idx, j, pending.pop(0), state)
        if j == n_blocks - 1:
            _, l_fin, acc = state
            o_t = acc * (1.0 / l_fin)
            outs += [o_t[:, t * tile:(t + 1) * tile] for t in range(HEADS_PER_PASS)]
            state = None
    o_ref[0] = jnp.concatenate(outs, axis=0).T.astype(BF16) * ga_ref[0]


def _attn_call(q, ga, k_l, k_c, vt_l, vt_c, *, tile):
    bsz, rows, _ = q.shape
    lc = k_c.shape[1]
    per_batch = lambda shape: pl.BlockSpec((1,) + shape, lambda b, j: (b, 0, 0))
    tok = pl.BlockSpec((1, tile, ATTN_WIDTH), lambda b, j: (b, j, 0))
    return pl.pallas_call(
        _attn_kernel,
        grid=(bsz, rows // tile),
        in_specs=[tok, tok,
                  per_batch((rows, KV_WIDTH)), per_batch((lc, KV_WIDTH)),
                  per_batch((KV_WIDTH, rows)), per_batch((KV_WIDTH, lc))],
        out_specs=pl.BlockSpec((1, tile, ATTN_WIDTH), lambda b, j: (b, j, 0)),
        out_shape=jax.ShapeDtypeStruct((bsz, rows, ATTN_WIDTH), BF16),
        compiler_params=pltpu.CompilerParams(dimension_semantics=("arbitrary", "arbitrary"),
                                             vmem_limit_bytes=VMEM_LIMIT),
        name="gqa_attention",
    )(q, ga, k_l, k_c, vt_l, vt_c)


def _conv_chunk(src_ref, dst_ref, cw_ref, cb_ref, shift_ref, i, n_chunks):
    i = jnp.asarray(i, jnp.int32)
    row0 = pl.multiple_of(i * CHUNK, CHUNK)
    main = src_ref[0, pl.ds(row0, CHUNK), :]
    prev0 = pl.multiple_of(jnp.maximum(row0 - CONV_HALO, 0), CONV_HALO)
    next0 = pl.multiple_of(jnp.minimum(row0 + CHUNK, (n_chunks - 1) * CHUNK + CHUNK - CONV_HALO), CONV_HALO)
    prev = src_ref[0, pl.ds(prev0, CONV_HALO), :] * (i > 0).astype(BF16)
    nxt = src_ref[0, pl.ds(next0, CONV_HALO), :] * (i < n_chunks - 1).astype(BF16)
    fill = jnp.zeros((CONV_WINDOW - CHUNK - 2 * CONV_HALO, CONV_CH), BF16)
    win = jnp.concatenate([prev, main, nxt, fill], axis=0)
    off_taps = [k for k in range(D_CONV) if k != CONV_PAD]
    for c0 in range(0, CONV_CH, GROUP_WIDTH):
        cols = slice(c0, c0 + GROUP_WIDTH)
        shifted = _dot(shift_ref[...], win[:, cols])
        acc = cb_ref[:, cols] + cw_ref[CONV_PAD:CONV_PAD + 1, cols] * main[:, cols].astype(F32)
        for j, k in enumerate(off_taps):
            acc = acc + cw_ref[k:k + 1, cols] * shifted[j * CHUNK:(j + 1) * CHUNK, :]
        dst_ref[pl.ds(row0, CHUNK), cols] = _silu(acc).astype(BF16)


def _pack3(v):
    p1, p2, p3 = _split3(v)
    packed = p1.astype(F32) + pltpu.roll(p2.astype(F32), PIECE_LANES, axis=1) \
        + pltpu.roll(p3.astype(F32), 2 * PIECE_LANES, axis=1)
    return packed.astype(BF16)


def _chunk_terms(dt_ref, terms, i, aneg2, tri2, lane_ok, lane_fwd, piece_fwd):
    acs_ref, src_t_ref, pk_end_ref, pk_start_ref = terms
    rows = pl.ds(pl.multiple_of(jnp.asarray(i, jnp.int32) * CHUNK, CHUNK), CHUNK)
    dtc = dt_ref[0, rows, :]
    r = _dot(tri2, _pack3(dtc * aneg2))
    r = jnp.where(piece_fwd, r[:CHUNK], r[CHUNK:])
    a_cs = (r + pltpu.roll(r, LANES - PIECE_LANES, axis=1) + pltpu.roll(r, LANES - 2 * PIECE_LANES, axis=1)) * lane_ok
    total = jnp.where(lane_fwd, a_cs[CHUNK - 1:CHUNK, :], a_cs[0:1, :])
    acs_ref[rows, :] = a_cs
    src_t_ref[rows, :] = (jnp.log2(dtc) - a_cs).T
    pk_end_ref[rows, :] = _pack3(dtc * jnp.exp2(total - a_cs) * lane_ok)
    pk_start_ref[rows, :] = _pack3(jnp.exp2(a_cs) * lane_ok)


def _chunk_cb(act_ref, cb_ref, i):
    rows = pl.ds(pl.multiple_of(jnp.asarray(i, jnp.int32) * CHUNK, CHUNK), CHUNK)
    for g in range(SSD_GROUPS):
        bg = act_ref[rows, SSD_WIDTH + g * D_STATE:SSD_WIDTH + (g + 1) * D_STATE]
        cg = act_ref[rows, SSD_WIDTH + GN + g * D_STATE:SSD_WIDTH + GN + (g + 1) * D_STATE]
        cb_ref[rows, g * CHUNK:(g + 1) * CHUNK] = lax.dot_general(
            cg, bg, (((1,), (1,)), ((), ())), preferred_element_type=F32)


def _ssd_chunk(act_ref, terms, cb_ref, row0, direction, want_y, expand, state_ref):
    acs_ref, src_t_ref, pk_end_ref, pk_start_ref = terms
    rows = pl.ds(row0, CHUNK)
    last = CHUNK - 1 if direction == 0 else 0
    a_cs = acs_ref[rows, :]
    dt_end_b = _dot(pk_end_ref[rows, :], expand)
    start_b = _dot(pk_start_ref[rows, :], expand)
    chunk_decay_b = start_b[last:last + 1, :]
    xs = act_ref[rows, :SSD_WIDTH].astype(F32)
    bm = act_ref[rows, SSD_WIDTH:SSD_WIDTH + GN]
    x_end = (xs * dt_end_b).astype(BF16)
    state = state_ref[...]
    made = [lax.dot_general(bm[:, g * D_STATE:(g + 1) * D_STATE],
                            x_end[:, g * GROUP_WIDTH:(g + 1) * GROUP_WIDTH],
                            (((0,), (0,)), ((), ())), preferred_element_type=F32)
            for g in range(SSD_GROUPS)]
    state_ref[...] = state * chunk_decay_b + jnp.concatenate(made, axis=1)
    if not want_y:
        return None
    cm = act_ref[rows, SSD_WIDTH + GN:]
    state_bf = state.astype(BF16)
    src_t = src_t_ref[rows, :]
    li = lax.broadcasted_iota(jnp.int32, (CHUNK, CHUNK), 0)
    si = lax.broadcasted_iota(jnp.int32, (CHUNK, CHUNK), 1)
    visible = (si <= li) if direction == 0 else (si >= li)
    slab_lane = lax.broadcasted_iota(jnp.int32, (CHUNK, KV_WIDTH), 1)
    heads_per_group = SSD_HEADS // SSD_GROUPS
    heads_per_slab = KV_WIDTH // SSD_HEAD_DIM
    slabs = []
    for g in range(SSD_GROUPS):
        cg = cm[:, g * D_STATE:(g + 1) * D_STATE]
        gcols = slice(g * GROUP_WIDTH, (g + 1) * GROUP_WIDTH)
        y_off = _dot(cg, state_bf[:, gcols]) * start_b[:, gcols]
        cb = cb_ref[rows, g * CHUNK:(g + 1) * CHUNK]
        for sl in range(GROUP_WIDTH // KV_WIDTH):
            slab0 = g * GROUP_WIDTH + sl * KV_WIDTH
            x_slab = act_ref[rows, slab0:slab0 + KV_WIDTH]
            acc = y_off[:, sl * KV_WIDTH:(sl + 1) * KV_WIDTH]
            for pair in range(heads_per_slab // 2):
                mats, xheads = [], []
                for t in (2 * pair, 2 * pair + 1):
                    hd = g * heads_per_group + sl * heads_per_slab + t
                    lane = direction * SSD_HEADS + hd
                    seg = a_cs[:, lane:lane + 1] + src_t[lane:lane + 1, :]
                    lmat = jnp.exp2(jnp.where(visible, seg, -jnp.inf))
                    mats.append((cb * lmat).astype(BF16))
                    in_head = (slab_lane >= t * SSD_HEAD_DIM) & (slab_lane < (t + 1) * SSD_HEAD_DIM)
                    xheads.append(jnp.where(in_head, x_slab, jnp.zeros_like(x_slab)))
                acc = acc + _dot(jnp.concatenate(mats, axis=1), jnp.concatenate(xheads, axis=0))
            slabs.append(acc)
    return jnp.concatenate(slabs, axis=1)


def _ssd_kernel(xl_ref, dtl_ref, xc_ref, dtc_ref, cw_ref, cb_ref, alog_ref, dskip_ref, shift_ref, tri2_ref,
                e_fwd_ref, e_bwd_ref, y_ref, act_l, act_c, y_part, state_f, state_b,
                acs_l, srct_l, pkend_l, pkstart_l, acs_c, srct_c, pkend_c, pkstart_c, cbt_l, *, n_lat, n_ctx):
    aneg2 = -jnp.exp(alog_ref[...]) * LOG2E
    dt_lane = lax.broadcasted_iota(jnp.int32, (1, DT_PAD), 1)
    lane_ok = (dt_lane < 2 * SSD_HEADS).astype(F32)
    lane_fwd = dt_lane < SSD_HEADS
    piece_fwd = (dt_lane % PIECE_LANES) < SSD_HEADS
    terms_l = (acs_l, srct_l, pkend_l, pkstart_l)
    terms_c = (acs_c, srct_c, pkend_c, pkstart_c)

    def prepare(src_ref, dt_ref, act_ref, terms, cbt_ref, i, n_chunks):
        _conv_chunk(src_ref, act_ref, cw_ref, cb_ref, shift_ref, i, n_chunks)
        _chunk_terms(dt_ref, terms, i, aneg2, tri2_ref[...], lane_ok, lane_fwd, piece_fwd)
        if cbt_ref is not None:
            _chunk_cb(act_ref, cbt_ref, i)

    lax.fori_loop(0, n_lat, lambda i, c: (prepare(xl_ref, dtl_ref, act_l, terms_l, cbt_l, i, n_lat), c)[1], 0,
                  unroll=PREP_UNROLL)
    for i in range(n_ctx):
        prepare(xc_ref, dtc_ref, act_c, terms_c, None, i, n_ctx)
    expands = (e_fwd_ref[...], e_bwd_ref[...])
    states = (state_f, state_b)
    for direction in range(2):
        states[direction][...] = jnp.zeros_like(states[direction])
        for i in range(n_ctx):
            chunk = i if direction == 0 else n_ctx - 1 - i
            _ssd_chunk(act_c, terms_c, None, chunk * CHUNK, direction, False, expands[direction], states[direction])

    def step(i, second_half):
        for direction in range(2):
            chunk = i if direction == 0 else n_lat - 1 - i
            row0 = pl.multiple_of(chunk * CHUNK, CHUNK)
            y = _ssd_chunk(act_l, terms_l, cbt_l, row0, direction, True, expands[direction], states[direction])
            rows = pl.ds(row0, CHUNK)
            if second_half:
                xs = act_l[rows, :SSD_WIDTH].astype(F32)
                y_ref[0, rows, :] = (y_part[rows, :] + y + dskip_ref[...] * xs).astype(BF16)
            else:
                y_part[rows, :] = y

    half = n_lat // 2
    lax.fori_loop(0, half, lambda i, c: (step(i, False), c)[1], 0, unroll=SCAN_UNROLL)
    lax.fori_loop(half, n_lat, lambda i, c: (step(i, True), c)[1], 0, unroll=SCAN_UNROLL)


def _ssd_call(xbc_l, dt_l, xbc_c, dt_c, consts):
    bsz, rows, _ = xbc_l.shape
    lc = xbc_c.shape[1]
    per_batch = lambda shape: pl.BlockSpec((1,) + shape, lambda b: (b, 0, 0))
    assert rows % (2 * CHUNK) == 0 and lc % CHUNK == 0
    body = functools.partial(_ssd_kernel, n_lat=rows // CHUNK, n_ctx=lc // CHUNK)
    return pl.pallas_call(
        body,
        grid=(bsz,),
        in_specs=[per_batch((rows, CONV_CH)), per_batch((rows, DT_PAD)),
                  per_batch((lc, CONV_CH)), per_batch((lc, DT_PAD)),
                  _const_spec((8, CONV_CH)), _const_spec((1, CONV_CH)), _const_spec((1, DT_PAD)),
                  _const_spec((1, SSD_WIDTH)), _const_spec(((D_CONV - 1) * CHUNK, CONV_WINDOW)),
                  _const_spec((2 * CHUNK, CHUNK)),
                  _const_spec((DT_PAD, SSD_WIDTH)), _const_spec((DT_PAD, SSD_WIDTH))],
        out_specs=per_batch((rows, SSD_WIDTH)),
        out_shape=jax.ShapeDtypeStruct((bsz, rows, SSD_WIDTH), BF16),
        scratch_shapes=[pltpu.VMEM((rows, CONV_CH), BF16), pltpu.VMEM((lc, CONV_CH), BF16),
                        pltpu.VMEM((rows, SSD_WIDTH), F32),
                        pltpu.VMEM((D_STATE, SSD_WIDTH), F32), pltpu.VMEM((D_STATE, SSD_WIDTH), F32)]
                       + [pltpu.VMEM((n, DT_PAD), dt) for n in (rows, lc) for dt in (F32, F32, BF16, BF16)]
                       + [pltpu.VMEM((rows, SSD_GROUPS * CHUNK), F32)],
        compiler_params=pltpu.CompilerParams(dimension_semantics=("arbitrary",),
                                             vmem_limit_bytes=VMEM_LIMIT),
        name="ssd_bidir",
    )(xbc_l, dt_l, xbc_c, dt_c, consts["conv_w"], consts["conv_b"], consts["a_log"], consts["d_skip"],
      consts["shift"], consts["tri2"], consts["e_fwd"], consts["e_bwd"])


def _out_kernel(ya_ref, y_ref, z_ref, x_ref, mod_ref, wa_ref, ws_ref, sg_ref, fg_ref, out_ref):
    n_sub = ya_ref.shape[1] // OUT_SUB
    gate = mod_ref[0][:, 2 * D_MODEL:]

    def gated(c):
        r = slice(c * OUT_SUB, (c + 1) * OUT_SUB)
        t = y_ref[0, r, :].astype(F32) * z_ref[0, r, :].astype(F32)
        y_s = (t * lax.rsqrt(jnp.mean(t * t, axis=-1, keepdims=True) + EPS) * sg_ref[...]).astype(BF16)
        return ya_ref[0, r, :], y_s

    def project(c, y_a, y_s):
        r = slice(c * OUT_SUB, (c + 1) * OUT_SUB)
        new = x_ref[0, r, :] + gate * (_dot(y_a, wa_ref[...]) + _dot(y_s, ws_ref[...]))
        out_ref[0, r, :] = new * lax.rsqrt(jnp.mean(new * new, axis=-1, keepdims=True) + EPS) * fg_ref[...]

    ready = gated(0)
    for c in range(n_sub):
        upcoming = gated(c + 1) if c + 1 < n_sub else None
        project(c, *ready)
        ready = upcoming


def _out_call(y_attn, y_ssd, z, x, mod, w_a, w_s, ssd_gain, final_gain, *, tile):
    bsz, rows, _ = x.shape
    tok = pl.BlockSpec((1, tile, D_MODEL), lambda b, j: (b, j, 0))
    return pl.pallas_call(
        _out_kernel,
        grid=(bsz, rows // tile),
        in_specs=[tok, tok, tok, tok,
                  pl.BlockSpec((1, 1, 3 * D_MODEL), lambda b, j: (b, 0, 0)),
                  _const_spec((ATTN_WIDTH, D_MODEL)), _const_spec((SSD_WIDTH, D_MODEL)),
                  _const_spec((1, SSD_WIDTH)), _const_spec((1, D_MODEL))],
        out_specs=tok,
        out_shape=jax.ShapeDtypeStruct((bsz, rows, D_MODEL), F32),
        compiler_params=pltpu.CompilerParams(dimension_semantics=("arbitrary", "arbitrary"),
                                             vmem_limit_bytes=VMEM_LIMIT),
        name="merge_out_proj",
    )(y_attn, y_ssd, z, x, mod, w_a, w_s, ssd_gain, final_gain)


def _rope_tables(rows):
    n_freq = HEAD_DIM // 4
    t = jnp.arange(rows, dtype=jnp.int32)
    pos = jnp.stack([(t // GRID_W).astype(F32), (t % GRID_W).astype(F32)], axis=1)
    inv_freq = ROPE_THETA ** (-jnp.arange(n_freq, dtype=F32) / n_freq)
    ang = pos[:, :, None] * inv_freq
    cos = jnp.cos(ang)[:, :, None, :]
    sin = jnp.sin(ang)[:, :, None, :]
    cos_h = jnp.broadcast_to(cos, (rows, 2, 2, n_freq)).reshape(rows, HEAD_DIM)
    sin_h = jnp.concatenate([-sin, sin], axis=2).reshape(rows, HEAD_DIM)
    reps = LANES // HEAD_DIM
    return jnp.tile(cos_h, (1, reps)), jnp.tile(sin_h, (1, reps))


def _pad_lanes(v, width):
    return jnp.pad(v, [(0, 0)] * (v.ndim - 1) + [(0, width - v.shape[-1])])


def kernel(x, c, ctx, c_ctx, ada_w, ada_b, norm_g, w_in, conv_w, conv_b, dt_bias, a_log, d_skip,
           q_norm_g, k_norm_g, ssd_norm_g, w_out, final_g):
    assert ada_w.shape[0] == 1, "single-layer problem: context outputs are never needed"
    bsz, rows, _ = x.shape
    ada_w, ada_b, norm_g, w_in, conv_w, conv_b = ada_w[0], ada_b[0], norm_g[0], w_in[0], conv_w[0], conv_b[0]
    dt_bias, a_log, d_skip = dt_bias[0], a_log[0], d_skip[0]
    q_norm_g, k_norm_g, ssd_norm_g, w_out = q_norm_g[0], k_norm_g[0], ssd_norm_g[0], w_out[0]

    n_rows = -(-(bsz + 1) // 8) * 8
    cc = jnp.concatenate([c, c_ctx[None, :], jnp.zeros((n_rows - bsz - 1, D_MODEL), F32)], axis=0)
    mod = _mod_call(cc, ada_w, ada_b)
    mod_l = mod[:bsz].reshape(bsz, 1, 3 * D_MODEL)
    mod_c = mod[bsz:bsz + 1].reshape(1, 1, 3 * D_MODEL)

    cols = lambda a, b: w_in[:, a:b].astype(BF16)
    weights = {
        "k": cols(0, OFF_V), "v": cols(OFF_V, OFF_XBC), "xbc": cols(OFF_XBC, OFF_DT),
        "dt": _pad_lanes(cols(OFF_DT, CTX_COLS), DT_PAD),
        "q": cols(OFF_Q, OFF_GA), "ga": cols(OFF_GA, OFF_Z), "z": cols(OFF_Z, w_in.shape[1]),
    }
    cos, sin = _rope_tables(rows)
    head_of_lane = np.arange(KV_WIDTH) // HEAD_DIM
    consts = {
        "dt_bias": _pad_lanes(dt_bias.reshape(1, 2 * SSD_HEADS), DT_PAD),
        "k_gain": jnp.tile(k_norm_g, KV_WIDTH // HEAD_DIM).reshape(1, KV_WIDTH),
        "q_gain": jnp.tile(q_norm_g, KV_WIDTH // HEAD_DIM).reshape(1, KV_WIDTH),
        "bd": jnp.asarray(head_of_lane[:, None] == head_of_lane[None, :], BF16),
        "cos": cos, "sin": sin,
    }
    ng = norm_g.reshape(1, D_MODEL)
    vt_c, k_c, xbc_c, dt_c = _in_proj_call(ctx, mod_c, ng, weights, consts, latent=False,
                                           tile=min(IN_TILE, ctx.shape[1]))
    vt_l, k_l, xbc_l, dt_l, q, ga, z = _in_proj_call(x, mod_l, ng, weights, consts, latent=True, tile=IN_TILE)

    y_attn = _attn_call(q, ga, k_l, k_c, vt_l, vt_c, tile=ATTN_TILE)

    tok = np.arange(CHUNK)
    lane_head = np.arange(SSD_WIDTH) // SSD_HEAD_DIM
    dt_lane = np.arange(DT_PAD)
    ssd_consts = {
        "conv_w": jnp.pad(conv_w, ((0, 8 - D_CONV), (0, 0))),
        "conv_b": conv_b.reshape(1, CONV_CH),
        "a_log": _pad_lanes(a_log.reshape(1, 2 * SSD_HEADS), DT_PAD),
        "d_skip": jnp.repeat(d_skip, SSD_HEAD_DIM).reshape(1, SSD_WIDTH),
        "shift": jnp.asarray(np.concatenate(
            [np.arange(CONV_WINDOW)[None, :] == (CONV_HALO + tok[:, None] + k - CONV_PAD)
             for k in range(D_CONV) if k != CONV_PAD], axis=0), BF16),
        "tri2": jnp.asarray(np.concatenate([tok[None, :] <= tok[:, None], tok[None, :] >= tok[:, None]], axis=0), BF16),
        "e_fwd": jnp.asarray((dt_lane[:, None] % PIECE_LANES == lane_head[None, :])
                             & (dt_lane[:, None] < 3 * PIECE_LANES), BF16),
        "e_bwd": jnp.asarray((dt_lane[:, None] % PIECE_LANES == lane_head[None, :] + SSD_HEADS)
                             & (dt_lane[:, None] < 3 * PIECE_LANES), BF16),
    }
    y_ssd = _ssd_call(xbc_l, dt_l, xbc_c, dt_c, ssd_consts)

    return _out_call(y_attn, y_ssd, z, x, mod_l, w_out[:ATTN_WIDTH].astype(BF16), w_out[ATTN_WIDTH:].astype(BF16),
                     ssd_norm_g.reshape(1, SSD_WIDTH), final_g.reshape(1, D_MODEL), tile=OUT_TILE)
```

```python
import functools

import jax
import jax.numpy as jnp
import numpy as np
from jax import lax
from jax.experimental import pallas as pl
from jax.experimental.pallas import tpu as pltpu

F32 = jnp.float32
BF16 = jnp.bfloat16

D_MODEL = 1024
GRID_W = 64
HEAD_DIM = 64
N_Q_HEADS = 16
N_KV_HEADS = 4
Q_PER_KV = N_Q_HEADS // N_KV_HEADS
ATTN_WIDTH = N_Q_HEADS * HEAD_DIM
KV_WIDTH = N_KV_HEADS * HEAD_DIM
SSD_HEADS = 16
SSD_HEAD_DIM = 64
SSD_WIDTH = SSD_HEADS * SSD_HEAD_DIM
SSD_GROUPS = 2
GROUP_WIDTH = SSD_WIDTH // SSD_GROUPS
D_STATE = 128
GN = SSD_GROUPS * D_STATE
D_CONV = 5
CONV_PAD = D_CONV // 2
CONV_CH = SSD_WIDTH + 2 * GN
CHUNK = 128
ROPE_THETA = 10000.0
ATTN_SCALE = HEAD_DIM ** -0.5
EPS = 1e-6
OFF_V = KV_WIDTH
OFF_XBC = 2 * KV_WIDTH
OFF_DT = OFF_XBC + CONV_CH
CTX_COLS = OFF_DT + 2 * SSD_HEADS
OFF_Q = CTX_COLS
OFF_GA = OFF_Q + ATTN_WIDTH
OFF_Z = OFF_GA + ATTN_WIDTH

LANES = 128
DT_PAD = LANES
CONV_HALO = 16
CONV_WINDOW = 256
PIECE_LANES = 2 * SSD_HEADS
LOG2E = 1.4426950408889634
VMEM_LIMIT = 56 * 1024 * 1024

IN_TILE = 1024
OUT_TILE = 1024
IN_SUB = 1024
OUT_SUB = 256
ATTN_TILE = 256
KEY_BLOCK = 256
LOOKAHEAD = 3
HEADS_PER_PASS = 2
PREP_UNROLL = 4
SCAN_UNROLL = 2


def _silu(v):
    return v * (1.0 / (1.0 + jnp.exp2(v * (-LOG2E))))


def _softplus(v):
    return jnp.maximum(v, 0.0) + jnp.log1p(jnp.exp(-jnp.abs(v)))


def _split3(v):
    p1 = v.astype(BF16)
    r1 = v - p1.astype(F32)
    p2 = r1.astype(BF16)
    p3 = (r1 - p2.astype(F32)).astype(BF16)
    return p1, p2, p3


def _dot(a, b):
    return jnp.dot(a, b, preferred_element_type=F32)


def _const_spec(shape):
    nd = len(shape)
    return pl.BlockSpec(shape, lambda *_: (0,) * nd, pipeline_mode=pl.Buffered(1))


def _mod_kernel(c_ref, w_ref, b_ref, o_ref):
    s = _silu(c_ref[...])
    o_ref[...] = jnp.dot(s, w_ref[...], preferred_element_type=F32,
                         precision=lax.Precision.HIGHEST) + b_ref[...]


def _mod_call(cc, ada_w, ada_b):
    rows = cc.shape[0]
    n = ada_w.shape[1]
    bn = D_MODEL
    return pl.pallas_call(
        _mod_kernel,
        grid=(n // bn,),
        in_specs=[pl.BlockSpec((rows, D_MODEL), lambda j: (0, 0)),
                  pl.BlockSpec((D_MODEL, bn), lambda j: (0, j)),
                  pl.BlockSpec((1, bn), lambda j: (0, j))],
        out_specs=pl.BlockSpec((rows, bn), lambda j: (0, j)),
        out_shape=jax.ShapeDtypeStruct((rows, n), F32),
        compiler_params=pltpu.CompilerParams(dimension_semantics=("arbitrary",),
                                             vmem_limit_bytes=VMEM_LIMIT),
        name="adaln_mod",
    )(cc, ada_w, ada_b.reshape(1, n))


def _modulated_norm(x, mod, ng):
    ms = jnp.mean(x * x, axis=-1, keepdims=True)
    y = (x * lax.rsqrt(ms + EPS)) * ng
    return y * (1.0 + mod[:, D_MODEL:2 * D_MODEL]) + mod[:, :D_MODEL]


def _head_norm(v, gain, bd):
    ssq = _dot((v * v).astype(BF16), bd)
    return v * lax.rsqrt(ssq * (1.0 / HEAD_DIM) + EPS) * gain


def _rope(v, cos, sin_signed):
    lane = lax.broadcasted_iota(jnp.int32, v.shape, 1)
    up = pltpu.roll(v, LANES - 16, axis=1)
    down = pltpu.roll(v, 16, axis=1)
    partner = jnp.where((lane & 16) == 0, up, down)
    return v * cos + partner * sin_signed


def _in_kernel(x_ref, mod_ref, ng_ref, wk_ref, wv_ref, wx_ref, wdt_ref, dtb_ref, kg_ref, bd_ref, *rest,
               latent, tile):
    if latent:
        wq_ref, wg_ref, wz_ref, qg_ref, cos_ref, sin_ref, vt_out, k_out, xbc_out, dt_out, q_out, ga_out, z_out = rest
        tile_row0 = pl.multiple_of(pl.program_id(1) * tile, tile)
    else:
        vt_out, k_out, xbc_out, dt_out = rest
    sub = min(IN_SUB, tile)

    def normed(c):
        return _modulated_norm(x_ref[0, c * sub:(c + 1) * sub, :], mod_ref[0], ng_ref[...]).astype(BF16)

    def rotated(v, cos, sin):
        if not latent:
            return v
        return jnp.concatenate([_rope(v[:, t * LANES:(t + 1) * LANES], cos, sin)
                                for t in range(KV_WIDTH // LANES)], axis=1)

    def project(c, h):
        r = slice(c * sub, (c + 1) * sub)
        cos = sin = None
        if latent:
            cos = cos_ref[pl.ds(tile_row0 + c * sub, sub), :]
            sin = sin_ref[pl.ds(tile_row0 + c * sub, sub), :]
        q_slabs = [slice(s * KV_WIDTH, (s + 1) * KV_WIDTH) for s in range(ATTN_WIDTH // KV_WIDTH)] if latent else []
        q_raw = [_dot(h, wq_ref[:, cols]) for cols in q_slabs]
        k_raw = _dot(h, wk_ref[...])
        xbc_out[0, r, :] = _dot(h, wx_ref[...]).astype(BF16)
        for cols, raw in zip(q_slabs, q_raw):
            qn = _head_norm(raw, qg_ref[...], bd_ref[...])
            q_out[0, r, cols] = (rotated(qn, cos, sin) * (ATTN_SCALE * LOG2E)).astype(BF16)
        kn = _head_norm(k_raw, kg_ref[...], bd_ref[...])
        k_out[0, r, :] = rotated(kn, cos, sin).astype(BF16)
        if latent:
            ga_out[0, r, :] = _silu(_dot(h, wg_ref[...])).astype(BF16)
            z_out[0, r, :] = _silu(_dot(h, wz_ref[...])).astype(BF16)
        vt_out[0, :, r] = _dot(h, wv_ref[...]).T.astype(BF16)
        dt_raw = _dot(h, wdt_ref[...]) + dtb_ref[...]
        lane = lax.broadcasted_iota(jnp.int32, dt_raw.shape, 1)
        dt_out[0, r, :] = jnp.where(lane < 2 * SSD_HEADS, _softplus(dt_raw), 0.0)

    ready = normed(0)
    for c in range(tile // sub):
        upcoming = normed(c + 1) if (c + 1) * sub < tile else None
        project(c, ready)
        ready = upcoming


def _in_proj_call(x, mod, ng, w, consts, *, latent, tile):
    bsz, rows, _ = x.shape
    nt = rows // tile
    tok = lambda width: pl.BlockSpec((1, tile, width), lambda b, j: (b, j, 0))
    mod_map = (lambda b, j: (b, 0, 0)) if latent else (lambda b, j: (0, 0, 0))
    in_specs = [tok(D_MODEL),
                pl.BlockSpec((1, 1, 3 * D_MODEL), mod_map),
                _const_spec((1, D_MODEL)),
                _const_spec((D_MODEL, KV_WIDTH)), _const_spec((D_MODEL, KV_WIDTH)),
                _const_spec((D_MODEL, CONV_CH)), _const_spec((D_MODEL, DT_PAD)),
                _const_spec((1, DT_PAD)), _const_spec((1, KV_WIDTH)), _const_spec((KV_WIDTH, KV_WIDTH))]
    args = [x, mod, ng, w["k"], w["v"], w["xbc"], w["dt"], consts["dt_bias"], consts["k_gain"], consts["bd"]]
    out_specs = [pl.BlockSpec((1, KV_WIDTH, tile), lambda b, j: (b, 0, j)),
                 tok(KV_WIDTH), tok(CONV_CH), tok(DT_PAD)]
    out_shape = [jax.ShapeDtypeStruct((bsz, KV_WIDTH, rows), BF16),
                 jax.ShapeDtypeStruct((bsz, rows, KV_WIDTH), BF16),
                 jax.ShapeDtypeStruct((bsz, rows, CONV_CH), BF16),
                 jax.ShapeDtypeStruct((bsz, rows, DT_PAD), F32)]
    if latent:
        in_specs += [_const_spec((D_MODEL, ATTN_WIDTH)), _const_spec((D_MODEL, ATTN_WIDTH)),
                     _const_spec((D_MODEL, SSD_WIDTH)), _const_spec((1, KV_WIDTH)),
                     _const_spec((rows, LANES)), _const_spec((rows, LANES))]
        args += [w["q"], w["ga"], w["z"], consts["q_gain"], consts["cos"], consts["sin"]]
        out_specs += [tok(ATTN_WIDTH), tok(ATTN_WIDTH), tok(SSD_WIDTH)]
        out_shape += [jax.ShapeDtypeStruct((bsz, rows, ATTN_WIDTH), BF16),
                      jax.ShapeDtypeStruct((bsz, rows, ATTN_WIDTH), BF16),
                      jax.ShapeDtypeStruct((bsz, rows, SSD_WIDTH), BF16)]
    body = functools.partial(_in_kernel, latent=latent, tile=tile)
    return pl.pallas_call(
        body,
        grid=(bsz, nt),
        in_specs=in_specs,
        out_specs=out_specs,
        out_shape=out_shape,
        compiler_params=pltpu.CompilerParams(dimension_semantics=("arbitrary", "arbitrary"),
                                             vmem_limit_bytes=VMEM_LIMIT),
        name="in_proj_latent" if latent else "in_proj_context",
    )(*args)


def _attn_kernel(q_ref, ga_ref, kl_ref, kc_ref, vtl_ref, vtc_ref, o_ref):
    tile = q_ref.shape[1]
    rows = kl_ref.shape[1]
    blocks = [(0, j * KEY_BLOCK, KEY_BLOCK) for j in range(rows // KEY_BLOCK)] + [(1, 0, kc_ref.shape[1])]
    n_blocks = len(blocks)
    n_pass = N_Q_HEADS // HEADS_PER_PASS
    q_t = q_ref[0].T
    zeros = jnp.zeros((HEAD_DIM, tile), BF16)

    def key_block(ref, transposed, kv, j):
        which, start, size = blocks[j]
        if transposed:
            return ref[which][0, kv * HEAD_DIM:(kv + 1) * HEAD_DIM, start:start + size]
        return ref[which][0, start:start + size, :]

    q_pads = {}

    def scores(pass_idx, j):
        kv = pass_idx * HEADS_PER_PASS // Q_PER_KV
        if pass_idx not in q_pads:
            q_pads[pass_idx] = jnp.concatenate(
                [jnp.concatenate([zeros] * kv + [q_t[hq * HEAD_DIM:(hq + 1) * HEAD_DIM, :]]
                                 + [zeros] * (N_KV_HEADS - 1 - kv), axis=0)
                 for hq in range(pass_idx * HEADS_PER_PASS, (pass_idx + 1) * HEADS_PER_PASS)], axis=1)
        return _dot(key_block((kl_ref, kc_ref), False, kv, j), q_pads[pass_idx])

    def absorb(pass_idx, j, s, state):
        kv = pass_idx * HEADS_PER_PASS // Q_PER_KV
        block_max = jnp.max(s, axis=0, keepdims=True)
        if state is None:
            m_new = block_max
        else:
            m_old, l_old, acc_old = state
            m_new = jnp.maximum(m_old, block_max)
            alpha = jnp.exp2(m_old - m_new)
        p = jnp.exp2(s - m_new)
        l_new = jnp.sum(p, axis=0, keepdims=True)
        acc = _dot(key_block((vtl_ref, vtc_ref), True, kv, j), p.astype(BF16))
        if state is not None:
            l_new = l_new + alpha * l_old
            acc = acc + alpha * acc_old
        return m_new, l_new, acc

    stages = [(p_, j) for p_ in range(n_pass) for j in range(n_blocks)]
    outs = []
    state = None
    pending = [scores(*stages[i]) for i in range(LOOKAHEAD)]
    for i, (pass_idx, j) in enumerate(stages):
        if i + LOOKAHEAD < len(stages):
            pending.append(scores(*stages[i + LOOKAHEAD]))
        state = absorb(pass_idx, j, pending.pop(0), state)
        if j == n_blocks - 1:
            _, l_fin, acc = state
            o_t = acc * (1.0 / l_fin)
            outs += [o_t[:, t * tile:(t + 1) * tile] for t in range(HEADS_PER_PASS)]
            state = None
    o_ref[0] = jnp.concatenate(outs, axis=0).T.astype(BF16) * ga_ref[0]


def _attn_call(q, ga, k_l, k_c, vt_l, vt_c, *, tile):
    bsz, rows, _ = q.shape
    lc = k_c.shape[1]
    per_batch = lambda shape: pl.BlockSpec((1,) + shape, lambda b, j: (b, 0, 0))
    tok = pl.BlockSpec((1, tile, ATTN_WIDTH), lambda b, j: (b, j, 0))
    return pl.pallas_call(
        _attn_kernel,
        grid=(bsz, rows // tile),
        in_specs=[tok, tok,
                  per_batch((rows, KV_WIDTH)), per_batch((lc, KV_WIDTH)),
                  per_batch((KV_WIDTH, rows)), per_batch((KV_WIDTH, lc))],
        out_specs=pl.BlockSpec((1, tile, ATTN_WIDTH), lambda b, j: (b, j, 0)),
        out_shape=jax.ShapeDtypeStruct((bsz, rows, ATTN_WIDTH), BF16),
        compiler_params=pltpu.CompilerParams(dimension_semantics=("arbitrary", "arbitrary"),
                                             vmem_limit_bytes=VMEM_LIMIT),
        name="gqa_attention",
    )(q, ga, k_l, k_c, vt_l, vt_c)


def _conv_chunk(src_ref, dst_ref, cw_ref, cb_ref, shift_ref, i, n_chunks):
    i = jnp.asarray(i, jnp.int32)
    row0 = pl.multiple_of(i * CHUNK, CHUNK)
    main = src_ref[0, pl.ds(row0, CHUNK), :]
    prev0 = pl.multiple_of(jnp.maximum(row0 - CONV_HALO, 0), CONV_HALO)
    next0 = pl.multiple_of(jnp.minimum(row0 + CHUNK, (n_chunks - 1) * CHUNK + CHUNK - CONV_HALO), CONV_HALO)
    prev = src_ref[0, pl.ds(prev0, CONV_HALO), :] * (i > 0).astype(BF16)
    nxt = src_ref[0, pl.ds(next0, CONV_HALO), :] * (i < n_chunks - 1).astype(BF16)
    fill = jnp.zeros((CONV_WINDOW - CHUNK - 2 * CONV_HALO, CONV_CH), BF16)
    win = jnp.concatenate([prev, main, nxt, fill], axis=0)
    off_taps = [k for k in range(D_CONV) if k != CONV_PAD]
    for c0 in range(0, CONV_CH, GROUP_WIDTH):
        cols = slice(c0, c0 + GROUP_WIDTH)
        shifted = _dot(shift_ref[...], win[:, cols])
        acc = cb_ref[:, cols] + cw_ref[CONV_PAD:CONV_PAD + 1, cols] * main[:, cols].astype(F32)
        for j, k in enumerate(off_taps):
            acc = acc + cw_ref[k:k + 1, cols] * shifted[j * CHUNK:(j + 1) * CHUNK, :]
        dst_ref[pl.ds(row0, CHUNK), cols] = _silu(acc).astype(BF16)


def _pack3(v):
    p1, p2, p3 = _split3(v)
    packed = p1.astype(F32) + pltpu.roll(p2.astype(F32), PIECE_LANES, axis=1) \
        + pltpu.roll(p3.astype(F32), 2 * PIECE_LANES, axis=1)
    return packed.astype(BF16)


def _chunk_terms(dt_ref, terms, i, aneg2, tri2, lane_ok, lane_fwd, piece_fwd):
    acs_ref, src_t_ref, pk_end_ref, pk_start_ref = terms
    rows = pl.ds(pl.multiple_of(jnp.asarray(i, jnp.int32) * CHUNK, CHUNK), CHUNK)
    dtc = dt_ref[0, rows, :]
    r = _dot(tri2, _pack3(dtc * aneg2))
    r = jnp.where(piece_fwd, r[:CHUNK], r[CHUNK:])
    a_cs = (r + pltpu.roll(r, LANES - PIECE_LANES, axis=1) + pltpu.roll(r, LANES - 2 * PIECE_LANES, axis=1)) * lane_ok
    total = jnp.where(lane_fwd, a_cs[CHUNK - 1:CHUNK, :], a_cs[0:1, :])
    acs_ref[rows, :] = a_cs
    src_t_ref[rows, :] = (jnp.log2(dtc) - a_cs).T
    pk_end_ref[rows, :] = _pack3(dtc * jnp.exp2(total - a_cs) * lane_ok)
    pk_start_ref[rows, :] = _pack3(jnp.exp2(a_cs) * lane_ok)


def _chunk_cb(act_ref, cb_ref, i):
    rows = pl.ds(pl.multiple_of(jnp.asarray(i, jnp.int32) * CHUNK, CHUNK), CHUNK)
    for g in range(SSD_GROUPS):
        bg = act_ref[rows, SSD_WIDTH + g * D_STATE:SSD_WIDTH + (g + 1) * D_STATE]
        cg = act_ref[rows, SSD_WIDTH + GN + g * D_STATE:SSD_WIDTH + GN + (g + 1) * D_STATE]
        cb_ref[rows, g * CHUNK:(g + 1) * CHUNK] = lax.dot_general(
            cg, bg, (((1,), (1,)), ((), ())), preferred_element_type=F32)


def _ssd_chunk(act_ref, terms, cb_ref, row0, direction, want_y, expand, state_ref):
    acs_ref, src_t_ref, pk_end_ref, pk_start_ref = terms
    rows = pl.ds(row0, CHUNK)
    last = CHUNK - 1 if direction == 0 else 0
    a_cs = acs_ref[rows, :]
    dt_end_b = _dot(pk_end_ref[rows, :], expand)
    start_b = _dot(pk_start_ref[rows, :], expand)
    chunk_decay_b = start_b[last:last + 1, :]
    xs = act_ref[rows, :SSD_WIDTH].astype(F32)
    bm = act_ref[rows, SSD_WIDTH:SSD_WIDTH + GN]
    x_end = (xs * dt_end_b).astype(BF16)
    state = state_ref[...]
    made = [lax.dot_general(bm[:, g * D_STATE:(g + 1) * D_STATE],
                            x_end[:, g * GROUP_WIDTH:(g + 1) * GROUP_WIDTH],
                            (((0,), (0,)), ((), ())), preferred_element_type=F32)
            for g in range(SSD_GROUPS)]
    state_ref[...] = state * chunk_decay_b + jnp.concatenate(made, axis=1)
    if not want_y:
        return None
    cm = act_ref[rows, SSD_WIDTH + GN:]
    state_bf = state.astype(BF16)
    src_t = src_t_ref[rows, :]
    li = lax.broadcasted_iota(jnp.int32, (CHUNK, CHUNK), 0)
    si = lax.broadcasted_iota(jnp.int32, (CHUNK, CHUNK), 1)
    visible = (si <= li) if direction == 0 else (si >= li)
    slab_lane = lax.broadcasted_iota(jnp.int32, (CHUNK, KV_WIDTH), 1)
    heads_per_group = SSD_HEADS // SSD_GROUPS
    heads_per_slab = KV_WIDTH // SSD_HEAD_DIM
    slabs = []
    for g in range(SSD_GROUPS):
        cg = cm[:, g * D_STATE:(g + 1) * D_STATE]
        gcols = slice(g * GROUP_WIDTH, (g + 1) * GROUP_WIDTH)
        y_off = _dot(cg, state_bf[:, gcols]) * start_b[:, gcols]
        cb = cb_ref[rows, g * CHUNK:(g + 1) * CHUNK]
        for sl in range(GROUP_WIDTH // KV_WIDTH):
            slab0 = g * GROUP_WIDTH + sl * KV_WIDTH
            x_slab = act_ref[rows, slab0:slab0 + KV_WIDTH]
            acc = y_off[:, sl * KV_WIDTH:(sl + 1) * KV_WIDTH]
            for pair in range(heads_per_slab // 2):
                mats, xheads = [], []
                for t in (2 * pair, 2 * pair + 1):
                    hd = g * heads_per_group + sl * heads_per_slab + t
                    lane = direction * SSD_HEADS + hd
                    seg = a_cs[:, lane:lane + 1] + src_t[lane:lane + 1, :]
                    lmat = jnp.exp2(jnp.where(visible, seg, -jnp.inf))
                    mats.append((cb * lmat).astype(BF16))
                    in_head = (slab_lane >= t * SSD_HEAD_DIM) & (slab_lane < (t + 1) * SSD_HEAD_DIM)
                    xheads.append(jnp.where(in_head, x_slab, jnp.zeros_like(x_slab)))
                acc = acc + _dot(jnp.concatenate(mats, axis=1), jnp.concatenate(xheads, axis=0))
            slabs.append(acc)
    return jnp.concatenate(slabs, axis=1)


def _ssd_kernel(xl_ref, dtl_ref, xc_ref, dtc_ref, cw_ref, cb_ref, alog_ref, dskip_ref, shift_ref, tri2_ref,
                e_fwd_ref, e_bwd_ref, y_ref, act_l, act_c, y_part, state_f, state_b,
                acs_l, srct_l, pkend_l, pkstart_l, acs_c, srct_c, pkend_c, pkstart_c, cbt_l, *, n_lat, n_ctx):
    aneg2 = -jnp.exp(alog_ref[...]) * LOG2E
    dt_lane = lax.broadcasted_iota(jnp.int32, (1, DT_PAD), 1)
    lane_ok = (dt_lane < 2 * SSD_HEADS).astype(F32)
    lane_fwd = dt_lane < SSD_HEADS
    piece_fwd = (dt_lane % PIECE_LANES) < SSD_HEADS
    terms_l = (acs_l, srct_l, pkend_l, pkstart_l)
    terms_c = (acs_c, srct_c, pkend_c, pkstart_c)

    def prepare(src_ref, dt_ref, act_ref, terms, cbt_ref, i, n_chunks):
        _conv_chunk(src_ref, act_ref, cw_ref, cb_ref, shift_ref, i, n_chunks)
        _chunk_terms(dt_ref, terms, i, aneg2, tri2_ref[...], lane_ok, lane_fwd, piece_fwd)
        if cbt_ref is not None:
            _chunk_cb(act_ref, cbt_ref, i)

    lax.fori_loop(0, n_lat, lambda i, c: (prepare(xl_ref, dtl_ref, act_l, terms_l, cbt_l, i, n_lat), c)[1], 0,
                  unroll=PREP_UNROLL)
    for i in range(n_ctx):
        prepare(xc_ref, dtc_ref, act_c, terms_c, None, i, n_ctx)
    expands = (e_fwd_ref[...], e_bwd_ref[...])
    states = (state_f, state_b)
    for direction in range(2):
        states[direction][...] = jnp.zeros_like(states[direction])
        for i in range(n_ctx):
            chunk = i if direction == 0 else n_ctx - 1 - i
            _ssd_chunk(act_c, terms_c, None, chunk * CHUNK, direction, False, expands[direction], states[direction])

    def step(i, second_half):
        for direction in range(2):
            chunk = i if direction == 0 else n_lat - 1 - i
            row0 = pl.multiple_of(chunk * CHUNK, CHUNK)
            y = _ssd_chunk(act_l, terms_l, cbt_l, row0, direction, True, expands[direction], states[direction])
            rows = pl.ds(row0, CHUNK)
            if second_half:
                xs = act_l[rows, :SSD_WIDTH].astype(F32)
                y_ref[0, rows, :] = (y_part[rows, :] + y + dskip_ref[...] * xs).astype(BF16)
            else:
                y_part[rows, :] = y

    half = n_lat // 2
    lax.fori_loop(0, half, lambda i, c: (step(i, False), c)[1], 0, unroll=SCAN_UNROLL)
    lax.fori_loop(half, n_lat, lambda i, c: (step(i, True), c)[1], 0, unroll=SCAN_UNROLL)


def _ssd_call(xbc_l, dt_l, xbc_c, dt_c, consts):
    bsz, rows, _ = xbc_l.shape
    lc = xbc_c.shape[1]
    per_batch = lambda shape: pl.BlockSpec((1,) + shape, lambda b: (b, 0, 0))
    assert rows % (2 * CHUNK) == 0 and lc % CHUNK == 0
    body = functools.partial(_ssd_kernel, n_lat=rows // CHUNK, n_ctx=lc // CHUNK)
    return pl.pallas_call(
        body,
        grid=(bsz,),
        in_specs=[per_batch((rows, CONV_CH)), per_batch((rows, DT_PAD)),
                  per_batch((lc, CONV_CH)), per_batch((lc, DT_PAD)),
                  _const_spec((8, CONV_CH)), _const_spec((1, CONV_CH)), _const_spec((1, DT_PAD)),
                  _const_spec((1, SSD_WIDTH)), _const_spec(((D_CONV - 1) * CHUNK, CONV_WINDOW)),
                  _const_spec((2 * CHUNK, CHUNK)),
                  _const_spec((DT_PAD, SSD_WIDTH)), _const_spec((DT_PAD, SSD_WIDTH))],
        out_specs=per_batch((rows, SSD_WIDTH)),
        out_shape=jax.ShapeDtypeStruct((bsz, rows, SSD_WIDTH), BF16),
        scratch_shapes=[pltpu.VMEM((rows, CONV_CH), BF16), pltpu.VMEM((lc, CONV_CH), BF16),
                        pltpu.VMEM((rows, SSD_WIDTH), F32),
                        pltpu.VMEM((D_STATE, SSD_WIDTH), F32), pltpu.VMEM((D_STATE, SSD_WIDTH), F32)]
                       + [pltpu.VMEM((n, DT_PAD), dt) for n in (rows, lc) for dt in (F32, F32, BF16, BF16)]
                       + [pltpu.VMEM((rows, SSD_GROUPS * CHUNK), F32)],
        compiler_params=pltpu.CompilerParams(dimension_semantics=("arbitrary",),
                                             vmem_limit_bytes=VMEM_LIMIT),
        name="ssd_bidir",
    )(xbc_l, dt_l, xbc_c, dt_c, consts["conv_w"], consts["conv_b"], consts["a_log"], consts["d_skip"],
      consts["shift"], consts["tri2"], consts["e_fwd"], consts["e_bwd"])


def _out_kernel(ya_ref, y_ref, z_ref, x_ref, mod_ref, wa_ref, ws_ref, sg_ref, fg_ref, out_ref):
    n_sub = ya_ref.shape[1] // OUT_SUB
    gate = mod_ref[0][:, 2 * D_MODEL:]

    def gated(c):
        r = slice(c * OUT_SUB, (c + 1) * OUT_SUB)
        t = y_ref[0, r, :].astype(F32) * z_ref[0, r, :].astype(F32)
        y_s = (t * lax.rsqrt(jnp.mean(t * t, axis=-1, keepdims=True) + EPS) * sg_ref[...]).astype(BF16)
        return ya_ref[0, r, :], y_s

    def project(c, y_a, y_s):
        r = slice(c * OUT_SUB, (c + 1) * OUT_SUB)
        new = x_ref[0, r, :] + gate * (_dot(y_a, wa_ref[...]) + _dot(y_s, ws_ref[...]))
        out_ref[0, r, :] = new * lax.rsqrt(jnp.mean(new * new, axis=-1, keepdims=True) + EPS) * fg_ref[...]

    ready = gated(0)
    for c in range(n_sub):
        upcoming = gated(c + 1) if c + 1 < n_sub else None
        project(c, *ready)
        ready = upcoming


def _out_call(y_attn, y_ssd, z, x, mod, w_a, w_s, ssd_gain, final_gain, *, tile):
    bsz, rows, _ = x.shape
    tok = pl.BlockSpec((1, tile, D_MODEL), lambda b, j: (b, j, 0))
    return pl.pallas_call(
        _out_kernel,
        grid=(bsz, rows // tile),
        in_specs=[tok, tok, tok, tok,
                  pl.BlockSpec((1, 1, 3 * D_MODEL), lambda b, j: (b, 0, 0)),
                  _const_spec((ATTN_WIDTH, D_MODEL)), _const_spec((SSD_WIDTH, D_MODEL)),
                  _const_spec((1, SSD_WIDTH)), _const_spec((1, D_MODEL))],
        out_specs=tok,
        out_shape=jax.ShapeDtypeStruct((bsz, rows, D_MODEL), F32),
        compiler_params=pltpu.CompilerParams(dimension_semantics=("arbitrary", "arbitrary"),
                                             vmem_limit_bytes=VMEM_LIMIT),
        name="merge_out_proj",
    )(y_attn, y_ssd, z, x, mod, w_a, w_s, ssd_gain, final_gain)


def _rope_tables(rows):
    n_freq = HEAD_DIM // 4
    t = jnp.arange(rows, dtype=jnp.int32)
    pos = jnp.stack([(t // GRID_W).astype(F32), (t % GRID_W).astype(F32)], axis=1)
    inv_freq = ROPE_THETA ** (-jnp.arange(n_freq, dtype=F32) / n_freq)
    ang = pos[:, :, None] * inv_freq
    cos = jnp.cos(ang)[:, :, None, :]
    sin = jnp.sin(ang)[:, :, None, :]
    cos_h = jnp.broadcast_to(cos, (rows, 2, 2, n_freq)).reshape(rows, HEAD_DIM)
    sin_h = jnp.concatenate([-sin, sin], axis=2).reshape(rows, HEAD_DIM)
    reps = LANES // HEAD_DIM
    return jnp.tile(cos_h, (1, reps)), jnp.tile(sin_h, (1, reps))


def _pad_lanes(v, width):
    return jnp.pad(v, [(0, 0)] * (v.ndim - 1) + [(0, width - v.shape[-1])])


def kernel(x, c, ctx, c_ctx, ada_w, ada_b, norm_g, w_in, conv_w, conv_b, dt_bias, a_log, d_skip,
           q_norm_g, k_norm_g, ssd_norm_g, w_out, final_g):
    assert ada_w.shape[0] == 1, "single-layer problem: context outputs are never needed"
    bsz, rows, _ = x.shape
    ada_w, ada_b, norm_g, w_in, conv_w, conv_b = ada_w[0], ada_b[0], norm_g[0], w_in[0], conv_w[0], conv_b[0]
    dt_bias, a_log, d_skip = dt_bias[0], a_log[0], d_skip[0]
    q_norm_g, k_norm_g, ssd_norm_g, w_out = q_norm_g[0], k_norm_g[0], ssd_norm_g[0], w_out[0]

    n_rows = -(-(bsz + 1) // 8) * 8
    cc = jnp.concatenate([c, c_ctx[None, :], jnp.zeros((n_rows - bsz - 1, D_MODEL), F32)], axis=0)
    mod = _mod_call(cc, ada_w, ada_b)
    mod_l = mod[:bsz].reshape(bsz, 1, 3 * D_MODEL)
    mod_c = mod[bsz:bsz + 1].reshape(1, 1, 3 * D_MODEL)

    cols = lambda a, b: w_in[:, a:b].astype(BF16)
    weights = {
        "k": cols(0, OFF_V), "v": cols(OFF_V, OFF_XBC), "xbc": cols(OFF_XBC, OFF_DT),
        "dt": _pad_lanes(cols(OFF_DT, CTX_COLS), DT_PAD),
        "q": cols(OFF_Q, OFF_GA), "ga": cols(OFF_GA, OFF_Z), "z": cols(OFF_Z, w_in.shape[1]),
    }
    cos, sin = _rope_tables(rows)
    head_of_lane = np.arange(KV_WIDTH) // HEAD_DIM
    consts = {
        "dt_bias": _pad_lanes(dt_bias.reshape(1, 2 * SSD_HEADS), DT_PAD),
        "k_gain": jnp.tile(k_norm_g, KV_WIDTH // HEAD_DIM).reshape(1, KV_WIDTH),
        "q_gain": jnp.tile(q_norm_g, KV_WIDTH // HEAD_DIM).reshape(1, KV_WIDTH),
        "bd": jnp.asarray(head_of_lane[:, None] == head_of_lane[None, :], BF16),
        "cos": cos, "sin": sin,
    }
    ng = norm_g.reshape(1, D_MODEL)
    vt_c, k_c, xbc_c, dt_c = _in_proj_call(ctx, mod_c, ng, weights, consts, latent=False,
                                           tile=min(IN_TILE, ctx.shape[1]))
    vt_l, k_l, xbc_l, dt_l, q, ga, z = _in_proj_call(x, mod_l, ng, weights, consts, latent=True, tile=IN_TILE)

    y_attn = _attn_call(q, ga, k_l, k_c, vt_l, vt_c, tile=ATTN_TILE)

    tok = np.arange(CHUNK)
    lane_head = np.arange(SSD_WIDTH) // SSD_HEAD_DIM
    dt_lane = np.arange(DT_PAD)
    ssd_consts = {
        "conv_w": jnp.pad(conv_w, ((0, 8 - D_CONV), (0, 0))),
        "conv_b": conv_b.reshape(1, CONV_CH),
        "a_log": _pad_lanes(a_log.reshape(1, 2 * SSD_HEADS), DT_PAD),
        "d_skip": jnp.repeat(d_skip, SSD_HEAD_DIM).reshape(1, SSD_WIDTH),
        "shift": jnp.asarray(np.concatenate(
            [np.arange(CONV_WINDOW)[None, :] == (CONV_HALO + tok[:, None] + k - CONV_PAD)
             for k in range(D_CONV) if k != CONV_PAD], axis=0), BF16),
        "tri2": jnp.asarray(np.concatenate([tok[None, :] <= tok[:, None], tok[None, :] >= tok[:, None]], axis=0), BF16),
        "e_fwd": jnp.asarray((dt_lane[:, None] % PIECE_LANES == lane_head[None, :])
                             & (dt_lane[:, None] < 3 * PIECE_LANES), BF16),
        "e_bwd": jnp.asarray((dt_lane[:, None] % PIECE_LANES == lane_head[None, :] + SSD_HEADS)
                             & (dt_lane[:, None] < 3 * PIECE_LANES), BF16),
    }
    y_ssd = _ssd_call(xbc_l, dt_l, xbc_c, dt_c, ssd_consts)

    return _out_call(y_attn, y_ssd, z, x, mod_l, w_out[:ATTN_WIDTH].astype(BF16), w_out[ATTN_WIDTH:].astype(BF16),
                     ssd_norm_g.reshape(1, SSD_WIDTH), final_g.reshape(1, D_MODEL), tile=OUT_TILE)
```
